```python
import jax, jax.numpy as jnp
from jax import lax
import numpy as np

D_MODEL = 1024
BATCH = 32
SEQ = 2048
DEPTH = 2
DEC_BATCH = 32
DEC_SEQ = 32
PAST_LEN = 1024

CHUNK = 64
D_MIX = D_MODEL
A_HEADS = 4
A_DV = (D_MIX // 2) // A_HEADS
A_DK = A_DV // 2
A_RANK = 16
A_TAU = 16.0
A_QK = A_HEADS * A_DK
A_VW = A_HEADS * A_DV
B_HEADS = 8
B_HD = (D_MIX // 2) // B_HEADS
B_W = B_HEADS * B_HD
B_LEFT_CHUNKS = 8
B_LEFT = B_LEFT_CHUNKS * CHUNK
REL_CLIP = 128
SPLITS = (A_QK, A_QK, A_VW, A_RANK, A_VW, B_W, B_W, B_W)
N_IN = A_QK + A_QK + A_VW + A_RANK + A_VW + B_W + B_W + B_W
D_FF = 2816
EPS = 1e-6

kernel_name = "hybrid_gla_chunkband_streaming_step"


def rmsnorm(x, g):
    x32 = x.astype(jnp.float32)
    y = x32 * lax.rsqrt(jnp.mean(x32 * x32, axis=-1, keepdims=True) + EPS)
    return (y * g.astype(jnp.float32)).astype(x.dtype)


def swiglu(h, wg, wu, wd):
    return (jax.nn.silu(h @ wg) * (h @ wu)) @ wd


def project(h, w_in, w_alpha, b_alpha):
    B, T, _ = h.shape
    z = h @ w_in
    offs = []
    acc = 0
    for s in SPLITS[:-1]:
        acc += s
        offs.append(acc)
    aq, ak, av, ag, ar, bq, bk, bv = jnp.split(z, offs, axis=-1)
    log_a = jax.nn.log_sigmoid((ag @ w_alpha + b_alpha).astype(jnp.float32)) / A_TAU
    hd = lambda t, n: t.reshape(B, T, n, -1)
    return (hd(aq, A_HEADS), hd(ak, A_HEADS), hd(av, A_HEADS), hd(log_a, A_HEADS), ar,
            hd(bq, B_HEADS), hd(bk, B_HEADS), hd(bv, B_HEADS))


def gla_chunked(q, k, v, log_a, s0):
    B, T, H, DK = q.shape
    DV = v.shape[-1]
    c = min(CHUNK, T)
    n = T // c
    f32 = jnp.float32
    q = q.astype(f32).reshape(B, n, c, H, DK) * (DK ** -0.5)
    k = k.astype(f32).reshape(B, n, c, H, DK)
    vv = v.astype(f32).reshape(B, n, c, H, DV)
    b = jnp.cumsum(log_a.astype(f32).reshape(B, n, c, H, DK), axis=2)
    b_last = b[:, :, -1:]
    q_dec = q * jnp.exp(b)
    scores = jnp.einsum('bnihd,bnjhd->bnhij', q_dec, k * jnp.exp(-b))
    causal = jnp.tril(jnp.ones((c, c), dtype=bool))
    scores = jnp.where(causal, scores, 0.0)
    o_intra = jnp.einsum('bnhij,bnjhv->bnihv', scores, vv)
    u = jnp.einsum('bnjhd,bnjhv->bnhdv', k * jnp.exp(b_last - b), vv)
    decay = jnp.exp(b_last[:, :, 0])

    def step(s, inp):
        d, du = inp
        return d[..., None] * s + du, s

    s_final, s_prev = lax.scan(step, s0.astype(f32),
                               (jnp.moveaxis(decay, 1, 0), jnp.moveaxis(u, 1, 0)))
    s_prev = jnp.moveaxis(s_prev, 0, 1)
    o_inter = jnp.einsum('bnihd,bnhdv->bnihv', q_dec, s_prev)
    o = (o_intra + o_inter).reshape(B, T, H, DV).astype(v.dtype)
    return o, s_final


def rel_bias(table, rel):
    return table[:, jnp.clip(rel, -REL_CLIP, REL_CLIP) + REL_CLIP]


def attend(q, k, v, bias, valid):
    s = jnp.einsum('bihd,bjhd->bhij', q, k).astype(jnp.float32) * (q.shape[-1] ** -0.5)
    s = s + bias[None].astype(jnp.float32)
    if valid is not None:
        s = jnp.where(valid, s, -1e30)
    p = jax.nn.softmax(s, axis=-1).astype(v.dtype)
    return jnp.einsum('bhij,bjhd->bihd', p, v)


def band_attn_prompt(q, k, v, bias_table):
    B, T, H, D = q.shape
    n = T // CHUNK
    KB = B_LEFT + CHUNK
    pad = ((0, 0), (B_LEFT, 0), (0, 0), (0, 0))
    kp = jnp.pad(k, pad)
    vp = jnp.pad(v, pad)
    rel = jnp.arange(CHUNK)[:, None] - jnp.arange(KB)[None, :] + B_LEFT
    bias = rel_bias(bias_table, rel)

    def one_chunk(ci):
        start = ci * CHUNK
        qc = lax.dynamic_slice_in_dim(q, start, CHUNK, axis=1)
        kc = lax.dynamic_slice_in_dim(kp, start, KB, axis=1)
        vc = lax.dynamic_slice_in_dim(vp, start, KB, axis=1)
        valid = (start - B_LEFT + jnp.arange(KB)) >= 0
        return attend(qc, kc, vc, bias, valid[None, :])

    out = lax.map(one_chunk, jnp.arange(n))
    return jnp.moveaxis(out, 0, 1).reshape(B, T, H, D)


def band_attn_sample(q, k, v, ck, cv, bias_table):
    T = q.shape[1]
    W = ck.shape[1]
    kk = jnp.concatenate([ck.astype(k.dtype), k], axis=1)
    vv = jnp.concatenate([cv.astype(v.dtype), v], axis=1)
    rel = jnp.arange(T)[:, None] - jnp.arange(W + T)[None, :] + W
    return attend(q, kk, vv, rel_bias(bias_table, rel), None)


def merge(o_gla, r, gla_norm, o_att, attn_norm, w_out):
    B, T = o_gla.shape[:2]
    g = rmsnorm(o_gla, gla_norm).reshape(B, T, A_VW) * jax.nn.silu(r)
    a = rmsnorm(o_att, attn_norm).reshape(B, T, B_W)
    return jnp.concatenate([g, a], axis=-1) @ w_out


def setup_inputs(seed: int = 0) -> dict:
    key = jax.random.key(seed)
    ks = jax.random.split(key, 24)
    nrm = lambda k, shape, scale: jax.random.normal(k, shape, jnp.float32) * scale
    W = min(B_LEFT, PAST_LEN)
    return {
        "x_prompt": nrm(ks[0], (BATCH, SEQ, D_MODEL), 1.0),
        "x_sample": nrm(ks[1], (DEC_BATCH, DEC_SEQ, D_MODEL), 1.0),
        "state_gla": nrm(ks[2], (DEPTH, DEC_BATCH, A_HEADS, A_DK, A_DV), 1.0),
        "cache_k": nrm(ks[3], (DEPTH, DEC_BATCH, W, B_HEADS, B_HD), 1.0),
        "cache_v": nrm(ks[4], (DEPTH, DEC_BATCH, W, B_HEADS, B_HD), 1.0),
        "norm_ffn1": 1.0 + nrm(ks[5], (DEPTH, D_MODEL), 0.02),
        "w_ffn1_gate": nrm(ks[6], (DEPTH, D_MODEL, D_FF), D_MODEL ** -0.5),
        "w_ffn1_up": nrm(ks[7], (DEPTH, D_MODEL, D_FF), D_MODEL ** -0.5),
        "w_ffn1_down": nrm(ks[8], (DEPTH, D_FF, D_MODEL), D_FF ** -0.5),
        "norm_mix": 1.0 + nrm(ks[9], (DEPTH, D_MODEL), 0.02),
        "w_in": nrm(ks[10], (DEPTH, D_MODEL, N_IN), D_MODEL ** -0.5),
        "w_alpha": nrm(ks[11], (DEPTH, A_RANK, A_QK), A_RANK ** -0.5),
        "b_alpha": nrm(ks[12], (DEPTH, A_QK), 0.1),
        "gla_norm": 1.0 + nrm(ks[13], (DEPTH, A_DV), 0.02),
        "attn_bias": nrm(ks[14], (DEPTH, B_HEADS, 2 * REL_CLIP + 1), 0.5),
        "attn_norm": 1.0 + nrm(ks[15], (DEPTH, B_HD), 0.02),
        "w_out": nrm(ks[16], (DEPTH, D_MIX, D_MODEL), D_MIX ** -0.5),
        "norm_ffn2": 1.0 + nrm(ks[17], (DEPTH, D_MODEL), 0.02),
        "w_ffn2_gate": nrm(ks[18], (DEPTH, D_MODEL, D_FF), D_MODEL ** -0.5),
        "w_ffn2_up": nrm(ks[19], (DEPTH, D_MODEL, D_FF), D_MODEL ** -0.5),
        "w_ffn2_down": nrm(ks[20], (DEPTH, D_FF, D_MODEL), D_FF ** -0.5),
        "final_norm": 1.0 + nrm(ks[21], (D_MODEL,), 0.02),
    }


def reference(x_prompt, x_sample, state_gla, cache_k, cache_v,
              norm_ffn1, w_ffn1_gate, w_ffn1_up, w_ffn1_down,
              norm_mix, w_in, w_alpha, b_alpha, gla_norm, attn_bias, attn_norm, w_out,
              norm_ffn2, w_ffn2_gate, w_ffn2_up, w_ffn2_down, final_norm):
    hp, hs = x_prompt, x_sample
    Bp = hp.shape[0]
    Wp = min(B_LEFT, hp.shape[1])
    gla_p, k_p, v_p, gla_s, k_s, v_s = [], [], [], [], [], []
    for l in range(DEPTH):
        hp = hp + 0.5 * swiglu(rmsnorm(hp, norm_ffn1[l]), w_ffn1_gate[l], w_ffn1_up[l], w_ffn1_down[l])
        hs = hs + 0.5 * swiglu(rmsnorm(hs, norm_ffn1[l]), w_ffn1_gate[l], w_ffn1_up[l], w_ffn1_down[l])

        aq, ak, av, la, ar, bq, bk, bv = project(rmsnorm(hp, norm_mix[l]), w_in[l], w_alpha[l], b_alpha[l])
        s0 = jnp.zeros((Bp, A_HEADS, A_DK, A_DV), jnp.float32)
        og, sp = gla_chunked(aq, ak, av, la, s0)
        oa = band_attn_prompt(bq, bk, bv, attn_bias[l])
        hp = hp + merge(og, ar, gla_norm[l], oa, attn_norm[l], w_out[l])
        gla_p.append(sp)
        k_p.append(bk[:, -Wp:])
        v_p.append(bv[:, -Wp:])

        aq, ak, av, la, ar, bq, bk, bv = project(rmsnorm(hs, norm_mix[l]), w_in[l], w_alpha[l], b_alpha[l])
        og, ss = gla_chunked(aq, ak, av, la, state_gla[l])
        oa = band_attn_sample(bq, bk, bv, cache_k[l], cache_v[l], attn_bias[l])
        hs = hs + merge(og, ar, gla_norm[l], oa, attn_norm[l], w_out[l])
        gla_s.append(ss)
        k_s.append(bk)
        v_s.append(bv)

        hp = hp + 0.5 * swiglu(rmsnorm(hp, norm_ffn2[l]), w_ffn2_gate[l], w_ffn2_up[l], w_ffn2_down[l])
        hs = hs + 0.5 * swiglu(rmsnorm(hs, norm_ffn2[l]), w_ffn2_gate[l], w_ffn2_up[l], w_ffn2_down[l])

    y_prompt = rmsnorm(hp, final_norm)
    y_sample = rmsnorm(hs, final_norm)
    return (y_prompt, y_sample,
            jnp.stack(gla_p), jnp.stack(k_p), jnp.stack(v_p),
            jnp.stack(gla_s), jnp.stack(k_s), jnp.stack(v_s))
```

```python
import functools

import jax
import jax.numpy as jnp
from jax import lax
from jax.experimental import pallas as pl
from jax.experimental.pallas import tpu as pltpu

F32 = jnp.float32
BF = jnp.bfloat16

D_MODEL = 1024
D_FF = 2816
DEPTH = 2
CHUNK = 64
A_HEADS = 4
A_DK = 64
A_DV = 128
A_QK = A_HEADS * A_DK
A_VW = A_HEADS * A_DV
A_RANK = 16
A_TAU = 16.0
B_HEADS = 8
B_HD = 64
B_W = B_HEADS * B_HD
B_LEFT = 8 * CHUNK
REL_CLIP = 128
EPS = 1e-6

TM = 512
FF_CHUNK = 256
GLA_STEP = 256
LANES = 128
BIAS_ROW = 768
VMEM_LIMIT = 56 * 1024 * 1024


def _rmsnorm(x, g):
    ms = jnp.mean(x * x, axis=-1, keepdims=True)
    return x * lax.rsqrt(ms + EPS) * g


def _silu(x):
    return x * jax.nn.sigmoid(x)


def _dot(a, b):
    return jnp.dot(a, b, preferred_element_type=F32)


def _dot_nt(a, b):
    return lax.dot_general(a, b, (((1,), (1,)), ((), ())), preferred_element_type=F32)


def _const_spec(shape):
    return pl.BlockSpec(shape, lambda *_: (0,) * len(shape), pipeline_mode=pl.Buffered(1))


def _ffn_kernel(*refs, mix, final):
    refs = list(refs)
    x_ref = refs.pop(0)
    if mix:
        g_ref, a_ref, wog_ref, woa_ref = refs[:4]
        refs = refs[4:]
    n_ref, wg_ref, wu_ref, wd_ref = refs[:4]
    refs = refs[4:]
    if final:
        fn_ref = refs.pop(0)
    o_ref, act_ref = refs

    x = x_ref[...]
    if mix:
        x = x + _dot(g_ref[...], wog_ref[...]) + _dot(a_ref[...], woa_ref[...])
    h = _rmsnorm(x, n_ref[...]).astype(BF)
    for c in range(D_FF // FF_CHUNK):
        sl = slice(c * FF_CHUNK, (c + 1) * FF_CHUNK)
        g = _dot(h, wg_ref[:, sl])
        u = _dot(h, wu_ref[:, sl])
        act_ref[:, sl] = (_silu(g) * u).astype(BF)
    y = x + 0.5 * _dot(act_ref[...], wd_ref[...])
    if final:
        y = _rmsnorm(y, fn_ref[...])
    o_ref[...] = y


def _ffn_call(x, norm, wg, wu, wd, mix=None, final_norm=None):
    n = x.shape[0]
    tok = lambda w: pl.BlockSpec((TM, w), lambda i: (i, 0))
    args = [x]
    specs = [tok(D_MODEL)]
    if mix is not None:
        g, a, wog, woa = mix
        args += [g, a, wog, woa]
        specs += [tok(A_VW), tok(B_W), _const_spec((A_VW, D_MODEL)), _const_spec((B_W, D_MODEL))]
    args += [norm, wg, wu, wd]
    specs += [_const_spec((1, D_MODEL)), _const_spec((D_MODEL, D_FF)), _const_spec((D_MODEL, D_FF)),
              _const_spec((D_FF, D_MODEL))]
    if final_norm is not None:
        args.append(final_norm)
        specs.append(_const_spec((1, D_MODEL)))
    return pl.pallas_call(
        functools.partial(_ffn_kernel, mix=mix is not None, final=final_norm is not None),
        grid=(n // TM,),
        in_specs=specs,
        out_specs=tok(D_MODEL),
        out_shape=jax.ShapeDtypeStruct((n, D_MODEL), F32),
        scratch_shapes=[pltpu.VMEM((TM, D_FF), BF)],
        compiler_params=pltpu.CompilerParams(dimension_semantics=("arbitrary",), vmem_limit_bytes=VMEM_LIMIT),
        name="ffn_mix" if mix is not None else "ffn",
    )(*args)


def _inproj_kernel(x_ref, n_ref, wa_ref, wgate_ref, wr_ref, wb_ref, walpha_ref, balpha_ref,
                   aq_ref, ak_ref, av_ref, la_ref, ar_ref, bq_ref, bk_ref, bv_ref, kt_ref, vt_ref,
                   *, tiles_per_stream, prompt_tiles):
    i = pl.program_id(0)
    h = _rmsnorm(x_ref[...], n_ref[...]).astype(BF)
    za = _dot(h, wa_ref[...])
    aq_ref[...] = za[:, :A_QK]
    ak_ref[...] = za[:, A_QK:2 * A_QK]
    av_ref[...] = za[:, 2 * A_QK:].astype(BF)
    ag = _dot(h, wgate_ref[...])
    pre = _dot(ag.astype(BF), walpha_ref[...]) + balpha_ref[...]
    log_sig = jnp.minimum(pre, 0.0) - jnp.log1p(jnp.exp(-jnp.abs(pre)))
    la_ref[...] = log_sig * (1.0 / A_TAU)
    ar_ref[...] = _dot(h, wr_ref[...])
    zb = _dot(h, wb_ref[...])
    bq_ref[...] = (zb[:, :B_W] * (B_HD ** -0.5)).astype(BF)
    k = zb[:, B_W:2 * B_W]
    v = zb[:, 2 * B_W:]
    bk_ref[...] = k.astype(BF)
    bv_ref[...] = v.astype(BF)

    @pl.when((i % tiles_per_stream == tiles_per_stream - 1) | (i >= prompt_tiles))
    def _():
        kt_ref[...] = k
        vt_ref[...] = v


def _inproj_call(x, norm, wa, wgate, wr, wb, walpha, balpha, *, n_prompt, seq):
    n = x.shape[0]
    tiles_per_stream = seq // TM
    prompt_tiles = n_prompt // TM
    n_streams = n_prompt // seq
    tail_rows = n_streams * B_LEFT + (n - n_prompt)
    tok = lambda w: pl.BlockSpec((TM, w), lambda i: (i, 0))
    tail = pl.BlockSpec(
        (TM, B_W),
        lambda i: (jnp.where(i < prompt_tiles, i // tiles_per_stream, i - prompt_tiles + n_streams), 0))
    outs = [
        (A_QK, F32), (A_QK, F32), (A_VW, BF), (A_QK, F32), (A_VW, F32), (B_W, BF), (B_W, BF), (B_W, BF)]
    return pl.pallas_call(
        functools.partial(_inproj_kernel, tiles_per_stream=tiles_per_stream, prompt_tiles=prompt_tiles),
        grid=(n // TM,),
        in_specs=[tok(D_MODEL), _const_spec((1, D_MODEL)), _const_spec(wa.shape), _const_spec(wgate.shape),
                  _const_spec(wr.shape), _const_spec(wb.shape), _const_spec(walpha.shape),
                  _const_spec(balpha.shape)],
        out_specs=[tok(w) for w, _ in outs] + [tail, tail],
        out_shape=[jax.ShapeDtypeStruct((n, w), dt) for w, dt in outs]
        + [jax.ShapeDtypeStruct((tail_rows, B_W), F32)] * 2,
        compiler_params=pltpu.CompilerParams(dimension_semantics=("arbitrary",), vmem_limit_bytes=VMEM_LIMIT),
        name="inproj",
    )(x, norm, wa, wgate, wr, wb, walpha, balpha)


def _gla_kernel(*refs, chunk, n_chunks, aliased):
    refs = list(refs)
    q_ref, k_ref, v_ref, la_ref, r_ref, s0_ref, gn_ref = refs[:7]
    refs = refs[7:]
    if aliased:
        refs.pop(0)
    g_ref, sout_ref, s_ref = refs
    j = pl.program_id(1)

    @pl.when(j == 0)
    def _():
        s_ref[...] = s0_ref[...]

    c = chunk
    row = lax.broadcasted_iota(jnp.int32, (c, c), 0)
    col = lax.broadcasted_iota(jnp.int32, (c, c), 1)
    causal = row >= col
    tri = jnp.where(causal, 1.0, 0.0).astype(BF)
    lane = lax.broadcasted_iota(jnp.int32, (c, A_QK), 1)
    head_mask = [(lane >= h * A_DK) & (lane < (h + 1) * A_DK) for h in range(A_HEADS)]
    gn = gn_ref[...]
    state = s_ref[...]

    for ci in range(n_chunks):
        rs = slice(ci * c, (ci + 1) * c)
        la = la_ref[rs, :]
        la_hi = la.astype(BF)
        rem = la - la_hi.astype(F32)
        la_mid = rem.astype(BF)
        la_lo = (rem - la_mid.astype(F32)).astype(BF)
        b = _dot(tri, la_hi) + _dot(tri, la_mid) + _dot(tri, la_lo)
        k = k_ref[rs, :]
        q_dec = (q_ref[rs, :] * (A_DK ** -0.5) * jnp.exp(b)).astype(BF)
        k_dec = (k * jnp.exp(-b)).astype(BF)
        q_bd = jnp.concatenate([jnp.where(head_mask[h], q_dec, jnp.zeros_like(q_dec))
                                for h in range(A_HEADS)], axis=0)
        scores = _dot_nt(q_bd, k_dec)
        o_inter = _dot(q_bd, state.astype(BF))
        k_t = k.T
        b_t = b.T
        b_last_t = b_t[:, c - 1:c]
        ku_t = (k_t * jnp.exp(b_last_t - b_t)).astype(BF)
        decay_t = jnp.exp(b_last_t)
        v = v_ref[rs, :]
        r = r_ref[rs, :]
        outs = []
        incs = []
        for h in range(A_HEADS):
            hs = slice(h * c, (h + 1) * c)
            vs = slice(h * A_DV, (h + 1) * A_DV)
            s_h = jnp.where(causal, scores[hs, :], 0.0).astype(BF)
            o = _dot(s_h, v[:, vs]) + o_inter[hs, :]
            incs.append(_dot(ku_t[h * A_DK:(h + 1) * A_DK, :], v[:, vs]))
            outs.append((_rmsnorm(o, gn) * _silu(r[:, vs])).astype(BF))
        g_ref[rs, :] = jnp.concatenate(outs, axis=1)
        state = decay_t * state + jnp.concatenate(incs, axis=0)

    s_ref[...] = state

    @pl.when(j == pl.num_programs(1) - 1)
    def _():
        sout_ref[...] = state


def _gla_call(aq, ak, av, la, ar, s0, gn, *, row0, seq, chunk, step, g_prev=None):
    n = aq.shape[0]
    n_streams = s0.shape[0]
    steps = seq // step
    base = row0 // step
    tok = lambda w: pl.BlockSpec((step, w), lambda b, j: (base + b * steps + j, 0))
    st = pl.BlockSpec((None, A_QK, A_DV), lambda b, j: (b, 0, 0))
    in_specs = [tok(A_QK), tok(A_QK), tok(A_VW), tok(A_QK), tok(A_VW), st,
                pl.BlockSpec((1, A_DV), lambda b, j: (0, 0))]
    args = [aq, ak, av, la, ar, s0, gn]
    aliases = {}
    if g_prev is not None:
        in_specs.append(pl.BlockSpec(memory_space=pl.ANY))
        args.append(g_prev)
        aliases = {len(args) - 1: 0}
    return pl.pallas_call(
        functools.partial(_gla_kernel, chunk=chunk, n_chunks=step // chunk, aliased=g_prev is not None),
        grid=(n_streams, steps),
        in_specs=in_specs,
        out_specs=[tok(A_VW), st],
        out_shape=[jax.ShapeDtypeStruct((n, A_VW), BF), jax.ShapeDtypeStruct((n_streams, A_QK, A_DV), F32)],
        scratch_shapes=[pltpu.VMEM((A_QK, A_DV), F32)],
        input_output_aliases=aliases,
        compiler_params=pltpu.CompilerParams(dimension_semantics=("arbitrary", "arbitrary"),
                                             vmem_limit_bytes=VMEM_LIMIT),
        name="gla_sample" if g_prev is not None else "gla_prompt",
    )(*args)


def _bias_block(brow_ref, blk, tq, tk):
    rows = []
    for h in range(4):
        f = jnp.broadcast_to(brow_ref[blk * 4 + h:blk * 4 + h + 1, :], (tq, BIAS_ROW))
        rows.append(pltpu.roll(f, BIAS_ROW - CHUNK, 1, stride=1, stride_axis=0)[:, :tk])
    return jnp.concatenate(rows, axis=0)


def _attend_block(qb, kw, vw, bias, valid, gain, diag, tq):
    lane = lax.broadcasted_iota(jnp.int32, qb.shape, 1)
    q_bd = jnp.concatenate(
        [jnp.where((lane >= h * B_HD) & (lane < (h + 1) * B_HD), qb, jnp.zeros_like(qb)) for h in range(4)],
        axis=0)
    s = _dot_nt(q_bd, kw) + bias
    if valid is not None:
        s = jnp.where(valid, s, -1e30)
    m = jnp.max(s, axis=-1, keepdims=True)
    p = jnp.exp(s - m)
    l = jnp.sum(p, axis=-1, keepdims=True)
    o = _dot(p.astype(BF), vw) / l
    o = jnp.where(diag, o, 0.0)
    ms = jnp.sum(o * o, axis=-1, keepdims=True) * (1.0 / B_HD)
    y = o * lax.rsqrt(ms + EPS) * gain
    return y[0:tq] + y[tq:2 * tq] + y[2 * tq:3 * tq] + y[3 * tq:4 * tq]


def _diag_mask(tq):
    r = lax.broadcasted_iota(jnp.int32, (4 * tq, 4 * B_HD), 0)
    cidx = lax.broadcasted_iota(jnp.int32, (4 * tq, 4 * B_HD), 1)
    m = None
    for h in range(4):
        t = (r >= h * tq) & (r < (h + 1) * tq) & (cidx >= h * B_HD) & (cidx < (h + 1) * B_HD)
        m = t if m is None else (m | t)
    return m


def _attn_prompt_kernel(q_ref, k_ref, v_ref, brow_ref, gain_ref, o_ref, kpad_ref, vpad_ref, bias_ref, *, seq):
    tk = B_LEFT + CHUNK
    kpad_ref[0:B_LEFT, :] = jnp.zeros((B_LEFT, B_W), BF)
    vpad_ref[0:B_LEFT, :] = jnp.zeros((B_LEFT, B_W), BF)
    kpad_ref[B_LEFT:, :] = k_ref[...]
    vpad_ref[B_LEFT:, :] = v_ref[...]
    for blk in range(2):
        bias_ref[blk] = _bias_block(brow_ref, blk, CHUNK, tk)
    gain = gain_ref[...]
    diag = _diag_mask(CHUNK)
    key_pos = lax.broadcasted_iota(jnp.int32, (1, tk), 1)

    def body(ci, carry):
        r0 = pl.multiple_of(ci * CHUNK, CHUNK)
        valid = key_pos + r0 >= B_LEFT
        for blk in range(2):
            cs = slice(blk * 4 * B_HD, (blk + 1) * 4 * B_HD)
            out = _attend_block(q_ref[pl.ds(r0, CHUNK), cs], kpad_ref[pl.ds(r0, tk), cs],
                                vpad_ref[pl.ds(r0, tk), cs], bias_ref[blk], valid, gain, diag, CHUNK)
            o_ref[pl.ds(r0, CHUNK), cs] = out.astype(BF)
        return carry

    lax.fori_loop(0, seq // CHUNK, body, 0)


def _attn_prompt_call(bq, bk, bv, brow, gain, *, n_streams, seq):
    n = bq.shape[0]
    tok = pl.BlockSpec((seq, B_W), lambda b: (b, 0))
    return pl.pallas_call(
        functools.partial(_attn_prompt_kernel, seq=seq),
        grid=(n_streams,),
        in_specs=[tok, tok, tok, pl.BlockSpec(brow.shape, lambda b: (0, 0)),
                  pl.BlockSpec(gain.shape, lambda b: (0, 0))],
        out_specs=tok,
        out_shape=jax.ShapeDtypeStruct((n, B_W), BF),
        scratch_shapes=[pltpu.VMEM((B_LEFT + seq, B_W), BF), pltpu.VMEM((B_LEFT + seq, B_W), BF),
                        pltpu.VMEM((2, 4 * CHUNK, B_LEFT + CHUNK), F32)],
        compiler_params=pltpu.CompilerParams(dimension_semantics=("arbitrary",), vmem_limit_bytes=VMEM_LIMIT),
        name="attn_prompt",
    )(bq, bk, bv, brow, gain)


def _attn_sample_kernel(q_ref, k_ref, v_ref, ck_ref, cv_ref, brow_ref, gain_ref, a_in_ref, o_ref, *, tq):
    del a_in_ref
    w = ck_ref.shape[0]
    gain = gain_ref[...]
    diag = _diag_mask(tq)
    for blk in range(2):
        cs = slice(blk * 4 * B_HD, (blk + 1) * 4 * B_HD)
        kw = jnp.concatenate([ck_ref[:, cs].astype(BF), k_ref[:, cs]], axis=0)
        vw = jnp.concatenate([cv_ref[:, cs].astype(BF), v_ref[:, cs]], axis=0)
        bias = _bias_block(brow_ref, blk, tq, w + tq)
        out = _attend_block(q_ref[:, cs], kw, vw, bias, None, gain, diag, tq)
        o_ref[:, cs] = out.astype(BF)


def _attn_sample_call(bq, bk, bv, ck, cv, brow, gain, a_prev, *, row0, tq):
    n_streams, w, _ = ck.shape
    base = row0 // tq
    tok = pl.BlockSpec((tq, B_W), lambda b: (base + b, 0))
    cache = pl.BlockSpec((None, w, B_W), lambda b: (b, 0, 0))
    return pl.pallas_call(
        functools.partial(_attn_sample_kernel, tq=tq),
        grid=(n_streams,),
        in_specs=[tok, tok, tok, cache, cache, pl.BlockSpec(brow.shape, lambda b: (0, 0)),
                  pl.BlockSpec(gain.shape, lambda b: (0, 0)), pl.BlockSpec(memory_space=pl.ANY)],
        out_specs=tok,
        out_shape=jax.ShapeDtypeStruct(a_prev.shape, BF),
        input_output_aliases={7: 0},
        compiler_params=pltpu.CompilerParams(dimension_semantics=("arbitrary",), vmem_limit_bytes=VMEM_LIMIT),
        name="attn_sample",
    )(bq, bk, bv, ck, cv, brow, gain, a_prev)


def kernel(x_prompt, x_sample, state_gla, cache_k, cache_v, norm_ffn1, w_ffn1_gate, w_ffn1_up, w_ffn1_down,
           norm_mix, w_in, w_alpha, b_alpha, gla_norm, attn_bias, attn_norm, w_out, norm_ffn2, w_ffn2_gate,
           w_ffn2_up, w_ffn2_down, final_norm):
    batch, seq, _ = x_prompt.shape
    dec_batch, dec_seq, _ = x_sample.shape
    n_prompt = batch * seq
    n_sample = dec_batch * dec_seq
    past = cache_k.shape[2]
    assert seq % TM == 0 and n_sample % TM == 0 and TM == B_LEFT and past == B_LEFT
    assert seq % GLA_STEP == 0 and dec_seq <= CHUNK

    x = jnp.concatenate([x_prompt.reshape(n_prompt, D_MODEL), x_sample.reshape(n_sample, D_MODEL)], axis=0)
    row = lambda v: v.reshape(1, -1).astype(F32)

    o_gate = 2 * A_QK + A_VW
    o_r = o_gate + A_RANK
    o_b = o_r + A_VW

    gla_p, k_p, v_p, gla_s, k_s, v_s = [], [], [], [], [], []
    zero_state = jnp.zeros((batch, A_QK, A_DV), F32)
    for l in range(DEPTH):
        bf = lambda w: w.astype(BF)
        x = _ffn_call(x, row(norm_ffn1[l]), bf(w_ffn1_gate[l]), bf(w_ffn1_up[l]), bf(w_ffn1_down[l]))

        wl = w_in[l]
        wgate = jnp.pad(wl[:, o_gate:o_r], ((0, 0), (0, LANES - A_RANK)))
        walpha = jnp.pad(w_alpha[l], ((0, LANES - A_RANK), (0, 0)))
        aq, ak, av, la, ar, bq, bk, bv, k_tail, v_tail = _inproj_call(
            x, row(norm_mix[l]), bf(wl[:, :o_gate]), bf(wgate), bf(wl[:, o_r:o_b]), bf(wl[:, o_b:]),
            bf(walpha), row(b_alpha[l]), n_prompt=n_prompt, seq=seq)

        gn = row(gla_norm[l])
        g, sp = _gla_call(aq, ak, av, la, ar, zero_state, gn, row0=0, seq=seq, chunk=CHUNK, step=GLA_STEP)
        g, ss = _gla_call(aq, ak, av, la, ar, state_gla[l].reshape(dec_batch, A_QK, A_DV), gn,
                          row0=n_prompt, seq=dec_seq, chunk=dec_seq, step=dec_seq, g_prev=g)

        brow = jnp.pad(attn_bias[l][:, ::-1],
                       ((0, 0), (B_LEFT - CHUNK, BIAS_ROW - (B_LEFT - CHUNK) - (2 * REL_CLIP + 1))), mode="edge")
        gain = jnp.tile(row(attn_norm[l]), (1, 4))
        a = _attn_prompt_call(bq, bk, bv, brow, gain, n_streams=batch, seq=seq)
        a = _attn_sample_call(bq, bk, bv, cache_k[l].reshape(dec_batch, past, B_W),
                              cache_v[l].reshape(dec_batch, past, B_W), brow, gain, a, row0=n_prompt, tq=dec_seq)

        wo = bf(w_out[l])
        x = _ffn_call(x, row(norm_ffn2[l]), bf(w_ffn2_gate[l]), bf(w_ffn2_up[l]), bf(w_ffn2_down[l]),
                      mix=(g, a, wo[:A_VW], wo[A_VW:]),
                      final_norm=row(final_norm) if l == DEPTH - 1 else None)

        n_tail = batch * B_LEFT
        gla_p.append(sp.reshape(batch, A_HEADS, A_DK, A_DV))
        gla_s.append(ss.reshape(dec_batch, A_HEADS, A_DK, A_DV))
        k_p.append(k_tail[:n_tail].reshape(batch, B_LEFT, B_HEADS, B_HD))
        v_p.append(v_tail[:n_tail].reshape(batch, B_LEFT, B_HEADS, B_HD))
        k_s.append(k_tail[n_tail:].reshape(dec_batch, dec_seq, B_HEADS, B_HD))
        v_s.append(v_tail[n_tail:].reshape(dec_batch, dec_seq, B_HEADS, B_HD))

    y_prompt = x[:n_prompt].reshape(batch, seq, D_MODEL)
    y_sample = x[n_prompt:].reshape(dec_batch, dec_seq, D_MODEL)
    return (y_prompt, y_sample, jnp.stack(gla_p), jnp.stack(k_p), jnp.stack(v_p),
            jnp.stack(gla_s), jnp.stack(k_s), jnp.stack(v_s))
```

```python
import functools

import jax
import jax.numpy as jnp
from jax import lax
from jax.experimental import pallas as pl
from jax.experimental.pallas import tpu as pltpu

F32 = jnp.float32
BF = jnp.bfloat16

D_MODEL = 1024
D_FF = 2816
DEPTH = 2
CHUNK = 64
A_HEADS = 4
A_DK = 64
A_DV = 128
A_QK = A_HEADS * A_DK
A_VW = A_HEADS * A_DV
A_RANK = 16
A_TAU = 16.0
B_HEADS = 8
B_HD = 64
B_W = B_HEADS * B_HD
B_LEFT = 8 * CHUNK
REL_CLIP = 128
EPS = 1e-6

TM = 512
FF_CHUNK = 256
GLA_STEP = 256
LANES = 128
BIAS_ROW = 768
BIAS_FLAT = B_LEFT - REL_CLIP
VMEM_LIMIT = 56 * 1024 * 1024


def _rmsnorm(x, g):
    ms = jnp.mean(x * x, axis=-1, keepdims=True)
    return x * lax.rsqrt(ms + EPS) * g


def _silu(x):
    return x * jax.nn.sigmoid(x)


def _dot(a, b):
    return jnp.dot(a, b, preferred_element_type=F32)


def _dot_nt(a, b):
    return lax.dot_general(a, b, (((1,), (1,)), ((), ())), preferred_element_type=F32)


def _const_spec(shape):
    return pl.BlockSpec(shape, lambda *_: (0,) * len(shape), pipeline_mode=pl.Buffered(1))


def _ffn_kernel(*refs, first, mix, final, prompt_tiles):
    refs = list(refs)
    i = pl.program_id(0)
    if first:
        xp_ref, xs_ref = refs[:2]
        refs = refs[2:]
        x = jnp.where(i < prompt_tiles, xp_ref[...], xs_ref[...])
    else:
        x = refs.pop(0)[...]
    if mix:
        g_ref, a_ref, wog_ref, woa_ref = refs[:4]
        refs = refs[4:]
        x = x + _dot(g_ref[...], wog_ref[...]) + _dot(a_ref[...], woa_ref[...])
    n_ref, wg_ref, wu_ref, wd_ref = refs[:4]
    refs = refs[4:]
    if final:
        fn_ref = refs.pop(0)
    act_ref = refs.pop()

    h = _rmsnorm(x, n_ref[...]).astype(BF)
    for c in range(D_FF // FF_CHUNK):
        sl = slice(c * FF_CHUNK, (c + 1) * FF_CHUNK)
        g = _dot(h, wg_ref[:, sl])
        u = _dot(h, wu_ref[:, sl])
        act_ref[:, sl] = (_silu(g) * u).astype(BF)
    y = x + 0.5 * _dot(act_ref[...], wd_ref[...])
    if not final:
        refs[0][...] = y
        return
    y = _rmsnorm(y, fn_ref[...])
    yp_ref, ys_ref = refs

    @pl.when(i < prompt_tiles)
    def _():
        yp_ref[...] = y

    @pl.when(i >= prompt_tiles)
    def _():
        ys_ref[...] = y


def _split_specs(width, prompt_tiles):
    return [pl.BlockSpec((TM, width), lambda i: (jnp.minimum(i, prompt_tiles - 1), 0)),
            pl.BlockSpec((TM, width), lambda i: (jnp.maximum(i - prompt_tiles, 0), 0))]


def _ffn_call(x, norm, wg, wu, wd, *, n_prompt, n_sample, mix=None, final_norm=None):
    n = n_prompt + n_sample
    prompt_tiles = n_prompt // TM
    first = isinstance(x, tuple)
    tok = lambda w: pl.BlockSpec((TM, w), lambda i: (i, 0))
    args = list(x) if first else [x]
    specs = _split_specs(D_MODEL, prompt_tiles) if first else [tok(D_MODEL)]
    if mix is not None:
        g, a, wog, woa = mix
        args += [g, a, wog, woa]
        specs += [tok(A_VW), tok(B_W), _const_spec((A_VW, D_MODEL)), _const_spec((B_W, D_MODEL))]
    args += [norm, wg, wu, wd]
    specs += [_const_spec((1, D_MODEL)), _const_spec((D_MODEL, D_FF)), _const_spec((D_MODEL, D_FF)),
              _const_spec((D_FF, D_MODEL))]
    if final_norm is not None:
        args.append(final_norm)
        specs.append(_const_spec((1, D_MODEL)))
        out_specs = _split_specs(D_MODEL, prompt_tiles)
        out_shape = [jax.ShapeDtypeStruct((n_prompt, D_MODEL), F32), jax.ShapeDtypeStruct((n_sample, D_MODEL), F32)]
    else:
        out_specs = tok(D_MODEL)
        out_shape = jax.ShapeDtypeStruct((n, D_MODEL), F32)
    return pl.pallas_call(
        functools.partial(_ffn_kernel, first=first, mix=mix is not None, final=final_norm is not None,
                          prompt_tiles=prompt_tiles),
        grid=(n // TM,),
        in_specs=specs,
        out_specs=out_specs,
        out_shape=out_shape,
        scratch_shapes=[pltpu.VMEM((TM, D_FF), BF)],
        compiler_params=pltpu.CompilerParams(dimension_semantics=("arbitrary",), vmem_limit_bytes=VMEM_LIMIT),
        name="ffn_mix" if mix is not None else "ffn",
    )(*args)


def _inproj_kernel(x_ref, n_ref, wa_ref, wgate_ref, wr_ref, wb_ref, walpha_ref, balpha_ref,
                   *refs, tiles_per_stream, prompt_tiles):
    aq_ref, ak_ref, av_ref, la_ref, ar_ref, bq_ref, bk_ref, bv_ref, kp_ref, vp_ref, ks_ref, vs_ref = refs[-12:]
    i = pl.program_id(0)
    h = _rmsnorm(x_ref[...], n_ref[...]).astype(BF)
    za = _dot(h, wa_ref[...])
    aq_ref[...] = za[:, :A_QK]
    ak_ref[...] = za[:, A_QK:2 * A_QK]
    av_ref[...] = za[:, 2 * A_QK:].astype(BF)
    ag = _dot(h, wgate_ref[...])
    pre = _dot(ag.astype(BF), walpha_ref[...]) + balpha_ref[...]
    log_sig = jnp.minimum(pre, 0.0) - jnp.log1p(jnp.exp(-jnp.abs(pre)))
    la_ref[...] = log_sig * (1.0 / A_TAU)
    ar_ref[...] = _dot(h, wr_ref[...])
    zb = _dot(h, wb_ref[...])
    bq_ref[...] = (zb[:, :B_W] * (B_HD ** -0.5)).astype(BF)
    k = zb[:, B_W:2 * B_W]
    v = zb[:, 2 * B_W:]
    bk_ref[...] = k.astype(BF)
    bv_ref[...] = v.astype(BF)

    @pl.when((i < prompt_tiles) & (i % tiles_per_stream == tiles_per_stream - 1))
    def _():
        kp_ref[...] = k
        vp_ref[...] = v

    @pl.when(i >= prompt_tiles)
    def _():
        ks_ref[...] = k
        vs_ref[...] = v


def _inproj_call(x, norm, wa, wgate, wr, wb, walpha, balpha, tails, *, layer, n_prompt, seq):
    n = x.shape[0]
    tiles_per_stream = seq // TM
    prompt_tiles = n_prompt // TM
    n_streams = n_prompt // seq
    tok = lambda w: pl.BlockSpec((TM, w), lambda i: (i, 0))
    p_tail = pl.BlockSpec((None, TM, B_W),
                          lambda i: (layer, jnp.minimum(i // tiles_per_stream, n_streams - 1), 0))
    s_tail = pl.BlockSpec((None, TM, B_W), lambda i: (layer, jnp.maximum(i - prompt_tiles, 0), 0))
    outs = [
        (A_QK, F32), (A_QK, F32), (A_VW, BF), (A_QK, F32), (A_VW, F32), (B_W, BF), (B_W, BF), (B_W, BF)]
    args = [x, norm, wa, wgate, wr, wb, walpha, balpha]
    in_specs = [tok(D_MODEL), _const_spec((1, D_MODEL)), _const_spec(wa.shape), _const_spec(wgate.shape),
                _const_spec(wr.shape), _const_spec(wb.shape), _const_spec(walpha.shape),
                _const_spec(balpha.shape)]
    aliases = {}
    if tails is not None:
        aliases = {len(args) + t: len(outs) + t for t in range(4)}
        args += list(tails)
        in_specs += [pl.BlockSpec(memory_space=pl.ANY)] * 4
    p_shape = jax.ShapeDtypeStruct((DEPTH, n_streams * B_LEFT, B_W), F32)
    s_shape = jax.ShapeDtypeStruct((DEPTH, n - n_prompt, B_W), F32)
    return pl.pallas_call(
        functools.partial(_inproj_kernel, tiles_per_stream=tiles_per_stream, prompt_tiles=prompt_tiles),
        grid=(n // TM,),
        in_specs=in_specs,
        out_specs=[tok(w) for w, _ in outs] + [p_tail, p_tail, s_tail, s_tail],
        out_shape=[jax.ShapeDtypeStruct((n, w), dt) for w, dt in outs] + [p_shape, p_shape, s_shape, s_shape],
        input_output_aliases=aliases,
        compiler_params=pltpu.CompilerParams(dimension_semantics=("arbitrary",), vmem_limit_bytes=VMEM_LIMIT),
        name="inproj",
    )(*args)


def _gla_kernel(*refs, chunk, n_chunks, aliased):
    refs = list(refs)
    q_ref, k_ref, v_ref, la_ref, r_ref, s0_ref, gn_ref = refs[:7]
    refs = refs[7:]
    if aliased:
        refs.pop(0)
    g_ref, sout_ref, s_ref = refs
    j = pl.program_id(1)

    @pl.when(j == 0)
    def _():
        s_ref[...] = s0_ref[...]

    c = chunk
    row = lax.broadcasted_iota(jnp.int32, (c, c), 0)
    col = lax.broadcasted_iota(jnp.int32, (c, c), 1)
    causal = row >= col
    tri = jnp.where(causal, 1.0, 0.0).astype(BF)
    lane = lax.broadcasted_iota(jnp.int32, (c, A_QK), 1)
    head_mask = [(lane >= h * A_DK) & (lane < (h + 1) * A_DK) for h in range(A_HEADS)]
    gn = gn_ref[...]
    state = s_ref[...]

    for ci in range(n_chunks):
        rs = slice(ci * c, (ci + 1) * c)
        la = la_ref[rs, :]
        la_hi = la.astype(BF)
        rem = la - la_hi.astype(F32)
        la_mid = rem.astype(BF)
        la_lo = (rem - la_mid.astype(F32)).astype(BF)
        b = _dot(tri, la_hi) + _dot(tri, la_mid) + _dot(tri, la_lo)
        k = k_ref[rs, :]
        q_dec = (q_ref[rs, :] * (A_DK ** -0.5) * jnp.exp(b)).astype(BF)
        k_dec = (k * jnp.exp(-b)).astype(BF)
        q_bd = jnp.concatenate([jnp.where(head_mask[h], q_dec, jnp.zeros_like(q_dec))
                                for h in range(A_HEADS)], axis=0)
        scores = _dot_nt(q_bd, k_dec)
        o_inter = _dot(q_bd, state.astype(BF))
        k_t = k.T
        b_t = b.T
        b_last_t = b_t[:, c - 1:c]
        ku_t = (k_t * jnp.exp(b_last_t - b_t)).astype(BF)
        decay_t = jnp.exp(b_last_t)
        v = v_ref[rs, :]
        r = r_ref[rs, :]
        outs = []
        incs = []
        for h in range(A_HEADS):
            hs = slice(h * c, (h + 1) * c)
            vs = slice(h * A_DV, (h + 1) * A_DV)
            s_h = jnp.where(causal, scores[hs, :], 0.0).astype(BF)
            o = _dot(s_h, v[:, vs]) + o_inter[hs, :]
            incs.append(_dot(ku_t[h * A_DK:(h + 1) * A_DK, :], v[:, vs]))
            outs.append((_rmsnorm(o, gn) * _silu(r[:, vs])).astype(BF))
        g_ref[rs, :] = jnp.concatenate(outs, axis=1)
        state = decay_t * state + jnp.concatenate(incs, axis=0)

    s_ref[...] = state

    @pl.when(j == pl.num_programs(1) - 1)
    def _():
        sout_ref[...] = state


def _gla_call(aq, ak, av, la, ar, s0, gn, *, row0, seq, chunk, step, g_prev=None):
    n = aq.shape[0]
    n_streams = s0.shape[0]
    steps = seq // step
    base = row0 // step
    tok = lambda w: pl.BlockSpec((step, w), lambda b, j: (base + b * steps + j, 0))
    st = pl.BlockSpec((None, A_QK, A_DV), lambda b, j: (b, 0, 0))
    in_specs = [tok(A_QK), tok(A_QK), tok(A_VW), tok(A_QK), tok(A_VW), st,
                pl.BlockSpec((1, A_DV), lambda b, j: (0, 0))]
    args = [aq, ak, av, la, ar, s0, gn]
    aliases = {}
    if g_prev is not None:
        in_specs.append(pl.BlockSpec(memory_space=pl.ANY))
        args.append(g_prev)
        aliases = {len(args) - 1: 0}
    return pl.pallas_call(
        functools.partial(_gla_kernel, chunk=chunk, n_chunks=step // chunk, aliased=g_prev is not None),
        grid=(n_streams, steps),
        in_specs=in_specs,
        out_specs=[tok(A_VW), st],
        out_shape=[jax.ShapeDtypeStruct((n, A_VW), BF), jax.ShapeDtypeStruct((n_streams, A_QK, A_DV), F32)],
        scratch_shapes=[pltpu.VMEM((A_QK, A_DV), F32)],
        input_output_aliases=aliases,
        compiler_params=pltpu.CompilerParams(dimension_semantics=("arbitrary", "arbitrary"),
                                             vmem_limit_bytes=VMEM_LIMIT),
        name="gla_sample" if g_prev is not None else "gla_prompt",
    )(*args)


def _bias_block(brow_ref, blk, tq, tk):
    rows = []
    for h in range(4):
        f = jnp.broadcast_to(brow_ref[blk * 4 + h:blk * 4 + h + 1, :], (tq, BIAS_ROW))
        rows.append(pltpu.roll(f, BIAS_ROW - CHUNK, 1, stride=1, stride_axis=0)[:, :tk])
    return jnp.concatenate(rows, axis=0)


def _block_diag_rows(qb):
    lane = lax.broadcasted_iota(jnp.int32, qb.shape, 1)
    return jnp.concatenate(
        [jnp.where((lane >= h * B_HD) & (lane < (h + 1) * B_HD), qb, jnp.zeros_like(qb)) for h in range(4)],
        axis=0)


def _attn_scores(qb, kw, valid):
    s = _dot_nt(_block_diag_rows(qb), kw)
    return s if valid is None else jnp.where(valid, s, -1e30)


def _attn_softmax(s, bias_tail):
    s = jnp.concatenate([s[:, :BIAS_FLAT], s[:, BIAS_FLAT:] + bias_tail], axis=1)
    p = jnp.exp(s - jnp.max(s, axis=-1, keepdims=True))
    l = jnp.sum(p, axis=-1, keepdims=True)
    return p.astype(BF), jnp.broadcast_to(l, (s.shape[0], LANES))


def _attn_values(p, l, vw):
    inv = 1.0 / l
    return _dot(p, vw) * jnp.concatenate([inv, inv], axis=1)


def _attn_norm(o, gain, diag, tq):
    o = jnp.where(diag, o, 0.0)
    ms = jnp.sum(o * o, axis=-1, keepdims=True) * (1.0 / B_HD)
    y = o * lax.rsqrt(ms + EPS) * gain
    return y[0:tq] + y[tq:2 * tq] + y[2 * tq:3 * tq] + y[3 * tq:4 * tq]


def _diag_mask(tq):
    r = lax.broadcasted_iota(jnp.int32, (4 * tq, 4 * B_HD), 0)
    cidx = lax.broadcasted_iota(jnp.int32, (4 * tq, 4 * B_HD), 1)
    m = None
    for h in range(4):
        t = (r >= h * tq) & (r < (h + 1) * tq) & (cidx >= h * B_HD) & (cidx < (h + 1) * B_HD)
        m = t if m is None else (m | t)
    return m


def _attn_prompt_kernel(q_ref, k_ref, v_ref, brow_ref, gain_ref, o_ref,
                        kpad_ref, vpad_ref, bias_ref, s_ref, p_ref, l_ref, acc_ref, *, seq):
    tk = B_LEFT + CHUNK
    n_chunks = seq // CHUNK
    kpad_ref[0:B_LEFT, :] = jnp.zeros((B_LEFT, B_W), BF)
    vpad_ref[0:B_LEFT, :] = jnp.zeros((B_LEFT, B_W), BF)
    kpad_ref[B_LEFT:, :] = k_ref[...]
    vpad_ref[B_LEFT:, :] = v_ref[...]
    for blk in range(2):
        bias_ref[blk] = _bias_block(brow_ref, blk, CHUNK, tk)[:, BIAS_FLAT:]
    gain = gain_ref[...]
    diag = _diag_mask(CHUNK)
    key_pos = lax.broadcasted_iota(jnp.int32, (1, tk), 1)

    blocks = [slice(blk * 4 * B_HD, (blk + 1) * 4 * B_HD) for blk in range(2)]

    def scores_stage(ci, slot, masked):
        r0 = pl.multiple_of(ci * CHUNK, CHUNK)
        valid = (key_pos + r0 >= B_LEFT) if masked else None
        for blk, cs in enumerate(blocks):
            s_ref[slot, blk] = _attn_scores(q_ref[pl.ds(r0, CHUNK), cs], kpad_ref[pl.ds(r0, tk), cs], valid)

    def softmax_stage(slot):
        for blk in range(2):
            p, l = _attn_softmax(s_ref[slot, blk], bias_ref[blk])
            p_ref[slot, blk] = p
            l_ref[slot, blk] = l

    def values_stage(ci, slot):
        r0 = pl.multiple_of(ci * CHUNK, CHUNK)
        for blk, cs in enumerate(blocks):
            acc_ref[slot, blk] = _attn_values(p_ref[slot, blk], l_ref[slot, blk], vpad_ref[pl.ds(r0, tk), cs])

    def norm_stage(ci, slot):
        r0 = pl.multiple_of(ci * CHUNK, CHUNK)
        for blk, cs in enumerate(blocks):
            o_ref[pl.ds(r0, CHUNK), cs] = _attn_norm(acc_ref[slot, blk], gain, diag, CHUNK).astype(BF)

    def static_step(i, masked):
        if i < n_chunks:
            scores_stage(i, i % 2, masked)
        if 0 <= i - 1 < n_chunks:
            softmax_stage((i - 1) % 2)
        if 0 <= i - 2 < n_chunks:
            values_stage(i - 2, i % 2)
        if 0 <= i - 3 < n_chunks:
            norm_stage(i - 3, (i - 1) % 2)

    def pair(masked):
        def body(j, carry):
            for d in range(2):
                i = 2 * j + d
                scores_stage(i, d, masked)
                softmax_stage(1 - d)
                values_stage(i - 2, d)
                norm_stage(i - 3, 1 - d)
            return carry
        return body

    n_masked = B_LEFT // CHUNK
    for i in range(4):
        static_step(i, True)
    lax.fori_loop(2, n_masked // 2, pair(True), 0)
    lax.fori_loop(n_masked // 2, n_chunks // 2, pair(False), 0)
    for i in range(n_chunks, n_chunks + 3):
        static_step(i, False)


def _attn_prompt_call(bq, bk, bv, brow, gain, *, n_streams, seq):
    n = bq.shape[0]
    tok = pl.BlockSpec((seq, B_W), lambda b: (b, 0))
    return pl.pallas_call(
        functools.partial(_attn_prompt_kernel, seq=seq),
        grid=(n_streams,),
        in_specs=[tok, tok, tok, pl.BlockSpec(brow.shape, lambda b: (0, 0)),
                  pl.BlockSpec(gain.shape, lambda b: (0, 0))],
        out_specs=tok,
        out_shape=jax.ShapeDtypeStruct((n, B_W), BF),
        scratch_shapes=[pltpu.VMEM((B_LEFT + seq, B_W), BF), pltpu.VMEM((B_LEFT + seq, B_W), BF),
                        pltpu.VMEM((2, 4 * CHUNK, B_LEFT + CHUNK - BIAS_FLAT), F32),
                        pltpu.VMEM((2, 2, 4 * CHUNK, B_LEFT + CHUNK), F32),
                        pltpu.VMEM((2, 2, 4 * CHUNK, B_LEFT + CHUNK), BF),
                        pltpu.VMEM((2, 2, 4 * CHUNK, LANES), F32),
                        pltpu.VMEM((2, 2, 4 * CHUNK, 4 * B_HD), F32)],
        compiler_params=pltpu.CompilerParams(dimension_semantics=("arbitrary",), vmem_limit_bytes=VMEM_LIMIT),
        name="attn_prompt",
    )(bq, bk, bv, brow, gain)


def _attn_sample_kernel(q_ref, k_ref, v_ref, ck_ref, cv_ref, brow_ref, gain_ref, a_in_ref, o_ref, *, tq):
    del a_in_ref
    w = ck_ref.shape[0]
    gain = gain_ref[...]
    diag = _diag_mask(tq)
    for blk in range(2):
        cs = slice(blk * 4 * B_HD, (blk + 1) * 4 * B_HD)
        kw = jnp.concatenate([ck_ref[:, cs].astype(BF), k_ref[:, cs]], axis=0)
        vw = jnp.concatenate([cv_ref[:, cs].astype(BF), v_ref[:, cs]], axis=0)
        bias_tail = _bias_block(brow_ref, blk, tq, w + tq)[:, BIAS_FLAT:]
        p, l = _attn_softmax(_attn_scores(q_ref[:, cs], kw, None), bias_tail)
        out = _attn_norm(_attn_values(p, l, vw), gain, diag, tq)
        o_ref[:, cs] = out.astype(BF)


def _attn_sample_call(bq, bk, bv, ck, cv, brow, gain, a_prev, *, row0, tq):
    n_streams, w, _ = ck.shape
    base = row0 // tq
    tok = pl.BlockSpec((tq, B_W), lambda b: (base + b, 0))
    cache = pl.BlockSpec((None, w, B_W), lambda b: (b, 0, 0))
    return pl.pallas_call(
        functools.partial(_attn_sample_kernel, tq=tq),
        grid=(n_streams,),
        in_specs=[tok, tok, tok, cache, cache, pl.BlockSpec(brow.shape, lambda b: (0, 0)),
                  pl.BlockSpec(gain.shape, lambda b: (0, 0)), pl.BlockSpec(memory_space=pl.ANY)],
        out_specs=tok,
        out_shape=jax.ShapeDtypeStruct(a_prev.shape, BF),
        input_output_aliases={7: 0},
        compiler_params=pltpu.CompilerParams(dimension_semantics=("arbitrary",), vmem_limit_bytes=VMEM_LIMIT),
        name="attn_sample",
    )(bq, bk, bv, ck, cv, brow, gain, a_prev)


def kernel(x_prompt, x_sample, state_gla, cache_k, cache_v, norm_ffn1, w_ffn1_gate, w_ffn1_up, w_ffn1_down,
           norm_mix, w_in, w_alpha, b_alpha, gla_norm, attn_bias, attn_norm, w_out, norm_ffn2, w_ffn2_gate,
           w_ffn2_up, w_ffn2_down, final_norm):
    batch, seq, _ = x_prompt.shape
    dec_batch, dec_seq, _ = x_sample.shape
    n_prompt = batch * seq
    n_sample = dec_batch * dec_seq
    past = cache_k.shape[2]
    assert seq % TM == 0 and n_sample % TM == 0 and TM == B_LEFT and past == B_LEFT
    assert seq % GLA_STEP == 0 and dec_seq <= CHUNK
    assert seq % (2 * CHUNK) == 0 and seq >= B_LEFT + 2 * CHUNK

    x = (x_prompt.reshape(n_prompt, D_MODEL), x_sample.reshape(n_sample, D_MODEL))
    row = lambda v: v.reshape(1, -1).astype(F32)
    sizes = dict(n_prompt=n_prompt, n_sample=n_sample)

    o_gate = 2 * A_QK + A_VW
    o_r = o_gate + A_RANK
    o_b = o_r + A_VW

    gla_p, gla_s = [], []
    tails = None
    zero_state = jnp.zeros((batch, A_QK, A_DV), F32)
    for l in range(DEPTH):
        bf = lambda w: w.astype(BF)
        x = _ffn_call(x, row(norm_ffn1[l]), bf(w_ffn1_gate[l]), bf(w_ffn1_up[l]), bf(w_ffn1_down[l]), **sizes)

        wl = w_in[l]
        wgate = jnp.pad(wl[:, o_gate:o_r], ((0, 0), (0, LANES - A_RANK)))
        walpha = jnp.pad(w_alpha[l], ((0, LANES - A_RANK), (0, 0)))
        aq, ak, av, la, ar, bq, bk, bv, *tails = _inproj_call(
            x, row(norm_mix[l]), bf(wl[:, :o_gate]), bf(wgate), bf(wl[:, o_r:o_b]), bf(wl[:, o_b:]),
            bf(walpha), row(b_alpha[l]), tails, layer=l, n_prompt=n_prompt, seq=seq)

        gn = row(gla_norm[l])
        g, sp = _gla_call(aq, ak, av, la, ar, zero_state, gn, row0=0, seq=seq, chunk=CHUNK, step=GLA_STEP)
        g, ss = _gla_call(aq, ak, av, la, ar, state_gla[l].reshape(dec_batch, A_QK, A_DV), gn,
                          row0=n_prompt, seq=dec_seq, chunk=dec_seq, step=dec_seq, g_prev=g)

        brow = jnp.pad(attn_bias[l][:, ::-1],
                       ((0, 0), (B_LEFT - CHUNK, BIAS_ROW - (B_LEFT - CHUNK) - (2 * REL_CLIP + 1))), mode="edge")
        brow = brow - brow[:, :1]
        gain = jnp.tile(row(attn_norm[l]), (1, 4))
        a = _attn_prompt_call(bq, bk, bv, brow, gain, n_streams=batch, seq=seq)
        a = _attn_sample_call(bq, bk, bv, cache_k[l].reshape(dec_batch, past, B_W),
                              cache_v[l].reshape(dec_batch, past, B_W), brow, gain, a, row0=n_prompt, tq=dec_seq)

        wo = bf(w_out[l])
        x = _ffn_call(x, row(norm_ffn2[l]), bf(w_ffn2_gate[l]), bf(w_ffn2_up[l]), bf(w_ffn2_down[l]),
                      mix=(g, a, wo[:A_VW], wo[A_VW:]),
                      final_norm=row(final_norm) if l == DEPTH - 1 else None, **sizes)
        gla_p.append(sp.reshape(batch, A_HEADS, A_DK, A_DV))
        gla_s.append(ss.reshape(dec_batch, A_HEADS, A_DK, A_DV))

    y_prompt, y_sample = x
    k_p, v_p, k_s, v_s = tails
    return (y_prompt.reshape(batch, seq, D_MODEL), y_sample.reshape(dec_batch, dec_seq, D_MODEL),
            jnp.stack(gla_p), k_p.reshape(DEPTH, batch, B_LEFT, B_HEADS, B_HD),
            v_p.reshape(DEPTH, batch, B_LEFT, B_HEADS, B_HD), jnp.stack(gla_s),
            k_s.reshape(DEPTH, dec_batch, dec_seq, B_HEADS, B_HD),
            v_s.reshape(DEPTH, dec_batch, dec_seq, B_HEADS, B_HD))
```

```python
import functools

import jax
import jax.numpy as jnp
from jax import lax
from jax.experimental import pallas as pl
from jax.experimental.pallas import tpu as pltpu

F32 = jnp.float32
BF = jnp.bfloat16

D_MODEL = 1024
D_FF = 2816
DEPTH = 2
CHUNK = 64
A_HEADS = 4
A_DK = 64
A_DV = 128
A_QK = A_HEADS * A_DK
A_VW = A_HEADS * A_DV
A_RANK = 16
A_TAU = 16.0
B_HEADS = 8
B_HD = 64
B_W = B_HEADS * B_HD
B_LEFT = 8 * CHUNK
REL_CLIP = 128
EPS = 1e-6

TM = 512
FF_CHUNK = 256
GLA_GROUP = 4
LANES = 128
BIAS_ROW = 768
BIAS_FLAT = B_LEFT - REL_CLIP
VMEM_LIMIT = 56 * 1024 * 1024


def _rmsnorm(x, g):
    ms = jnp.mean(x * x, axis=-1, keepdims=True)
    return x * lax.rsqrt(ms + EPS) * g


def _silu(x):
    return x * jax.nn.sigmoid(x)


def _dot(a, b):
    return jnp.dot(a, b, preferred_element_type=F32)


def _dot_nt(a, b):
    return lax.dot_general(a, b, (((1,), (1,)), ((), ())), preferred_element_type=F32)


def _const_spec(shape):
    return pl.BlockSpec(shape, lambda *_: (0,) * len(shape), pipeline_mode=pl.Buffered(1))


def _ffn_kernel(*refs, first, mix, final, prompt_tiles):
    refs = list(refs)
    i = pl.program_id(0)
    if first:
        xp_ref, xs_ref = refs[:2]
        refs = refs[2:]
        x = jnp.where(i < prompt_tiles, xp_ref[...], xs_ref[...])
    else:
        x = refs.pop(0)[...]
    if mix:
        gp_ref, gs_ref, ap_ref, as_ref, wog_ref, woa_ref = refs[:6]
        refs = refs[6:]
        g = jnp.where(i < prompt_tiles, gp_ref[...], gs_ref[...])
        a = jnp.where(i < prompt_tiles, ap_ref[...], as_ref[...])
        x = x + _dot(g, wog_ref[...]) + _dot(a, woa_ref[...])
    n_ref, wg_ref, wu_ref, wd_ref = refs[:4]
    refs = refs[4:]
    if final:
        fn_ref = refs.pop(0)
    act_ref = refs.pop()

    h = _rmsnorm(x, n_ref[...]).astype(BF)
    for c in range(D_FF // FF_CHUNK):
        sl = slice(c * FF_CHUNK, (c + 1) * FF_CHUNK)
        g = _dot(h, wg_ref[:, sl])
        u = _dot(h, wu_ref[:, sl])
        act_ref[:, sl] = (_silu(g) * u).astype(BF)
    y = x + 0.5 * _dot(act_ref[...], wd_ref[...])
    if not final:
        refs[0][...] = y
        return
    y = _rmsnorm(y, fn_ref[...])
    yp_ref, ys_ref = refs

    @pl.when(i < prompt_tiles)
    def _():
        yp_ref[...] = y

    @pl.when(i >= prompt_tiles)
    def _():
        ys_ref[...] = y


def _split_specs(width, prompt_tiles):
    return [pl.BlockSpec((TM, width), lambda i: (jnp.minimum(i, prompt_tiles - 1), 0)),
            pl.BlockSpec((TM, width), lambda i: (jnp.maximum(i - prompt_tiles, 0), 0))]


def _ffn_call(x, norm, wg, wu, wd, *, n_prompt, n_sample, mix=None, final_norm=None):
    n = n_prompt + n_sample
    prompt_tiles = n_prompt // TM
    first = isinstance(x, tuple)
    tok = lambda w: pl.BlockSpec((TM, w), lambda i: (i, 0))
    args = list(x) if first else [x]
    specs = _split_specs(D_MODEL, prompt_tiles) if first else [tok(D_MODEL)]
    if mix is not None:
        g, a, wog, woa = mix
        args += [*g, *a, wog, woa]
        specs += _split_specs(A_VW, prompt_tiles) + _split_specs(B_W, prompt_tiles)
        specs += [_const_spec((A_VW, D_MODEL)), _const_spec((B_W, D_MODEL))]
    args += [norm, wg, wu, wd]
    specs += [_const_spec((1, D_MODEL)), _const_spec((D_MODEL, D_FF)), _const_spec((D_MODEL, D_FF)),
              _const_spec((D_FF, D_MODEL))]
    if final_norm is not None:
        args.append(final_norm)
        specs.append(_const_spec((1, D_MODEL)))
        out_specs = _split_specs(D_MODEL, prompt_tiles)
        out_shape = [jax.ShapeDtypeStruct((n_prompt, D_MODEL), F32), jax.ShapeDtypeStruct((n_sample, D_MODEL), F32)]
    else:
        out_specs = tok(D_MODEL)
        out_shape = jax.ShapeDtypeStruct((n, D_MODEL), F32)
    return pl.pallas_call(
        functools.partial(_ffn_kernel, first=first, mix=mix is not None, final=final_norm is not None,
                          prompt_tiles=prompt_tiles),
        grid=(n // TM,),
        in_specs=specs,
        out_specs=out_specs,
        out_shape=out_shape,
        scratch_shapes=[pltpu.VMEM((TM, D_FF), BF)],
        compiler_params=pltpu.CompilerParams(dimension_semantics=("arbitrary",), vmem_limit_bytes=VMEM_LIMIT),
        name="ffn_mix" if mix is not None else "ffn",
    )(*args)


def _inproj_kernel(x_ref, n_ref, wa_ref, wgate_ref, wr_ref, wb_ref, walpha_ref, balpha_ref,
                   aq_ref, ak_ref, av_ref, la_ref, ar_ref, bq_ref, bk_ref, bv_ref, kp_ref, vp_ref, ks_ref, vs_ref,
                   *, tiles_per_stream, prompt_tiles):
    i = pl.program_id(0)
    h = _rmsnorm(x_ref[...], n_ref[...]).astype(BF)
    za = _dot(h, wa_ref[...])
    aq_ref[...] = za[:, :A_QK]
    ak_ref[...] = za[:, A_QK:2 * A_QK]
    av_ref[...] = za[:, 2 * A_QK:].astype(BF)
    ag = _dot(h, wgate_ref[...])
    pre = _dot(ag.astype(BF), walpha_ref[...]) + balpha_ref[...]
    log_sig = jnp.minimum(pre, 0.0) - jnp.log1p(jnp.exp(-jnp.abs(pre)))
    la_ref[...] = log_sig * (1.0 / A_TAU)
    ar_ref[...] = _dot(h, wr_ref[...])
    zb = _dot(h, wb_ref[...])
    bq_ref[...] = (zb[:, :B_W] * (B_HD ** -0.5)).astype(BF)
    k = zb[:, B_W:2 * B_W]
    v = zb[:, 2 * B_W:]
    bk_ref[...] = k.astype(BF)
    bv_ref[...] = v.astype(BF)

    @pl.when((i < prompt_tiles) & (i % tiles_per_stream == tiles_per_stream - 1))
    def _():
        kp_ref[...] = k
        vp_ref[...] = v

    @pl.when(i >= prompt_tiles)
    def _():
        ks_ref[...] = k
        vs_ref[...] = v


def _inproj_call(x, norm, wa, wgate, wr, wb, walpha, balpha, *, n_prompt, seq):
    n = x.shape[0]
    tiles_per_stream = seq // TM
    prompt_tiles = n_prompt // TM
    n_streams = n_prompt // seq
    tok = lambda w: pl.BlockSpec((TM, w), lambda i: (i, 0))
    p_tail = pl.BlockSpec((TM, B_W), lambda i: (jnp.minimum(i // tiles_per_stream, n_streams - 1), 0))
    s_tail = pl.BlockSpec((TM, B_W), lambda i: (jnp.maximum(i - prompt_tiles, 0), 0))
    outs = [
        (A_QK, F32), (A_QK, F32), (A_VW, BF), (A_QK, F32), (A_VW, F32), (B_W, BF), (B_W, BF), (B_W, BF)]
    p_shape = jax.ShapeDtypeStruct((n_streams * B_LEFT, B_W), F32)
    s_shape = jax.ShapeDtypeStruct((n - n_prompt, B_W), F32)
    return pl.pallas_call(
        functools.partial(_inproj_kernel, tiles_per_stream=tiles_per_stream, prompt_tiles=prompt_tiles),
        grid=(n // TM,),
        in_specs=[tok(D_MODEL), _const_spec((1, D_MODEL)), _const_spec(wa.shape), _const_spec(wgate.shape),
                  _const_spec(wr.shape), _const_spec(wb.shape), _const_spec(walpha.shape),
                  _const_spec(balpha.shape)],
        out_specs=[tok(w) for w, _ in outs] + [p_tail, p_tail, s_tail, s_tail],
        out_shape=[jax.ShapeDtypeStruct((n, w), dt) for w, dt in outs] + [p_shape, p_shape, s_shape, s_shape],
        compiler_params=pltpu.CompilerParams(dimension_semantics=("arbitrary",), vmem_limit_bytes=VMEM_LIMIT),
        name="inproj",
    )(x, norm, wa, wgate, wr, wb, walpha, balpha)


def _gla_kernel(q_ref, k_ref, v_ref, la_ref, r_ref, s0_ref, gn_ref, g_ref, sout_ref,
                qbd_ref, kdec_ref, inc_ref, dec_ref, s_ref, *, chunk, group, n_sub):
    c = chunk
    rows = group * c
    r_i = lax.broadcasted_iota(jnp.int32, (rows, rows), 0)
    c_i = lax.broadcasted_iota(jnp.int32, (rows, rows), 1)
    same_chunk = (r_i & -c) == (c_i & -c)
    tri = jnp.where(same_chunk & (r_i >= c_i), 1.0, 0.0).astype(BF)
    causal = lax.broadcasted_iota(jnp.int32, (c, c), 0) >= lax.broadcasted_iota(jnp.int32, (c, c), 1)
    lane = lax.broadcasted_iota(jnp.int32, (c, A_QK), 1)
    head_mask = [(lane >= h * A_DK) & (lane < (h + 1) * A_DK) for h in range(A_HEADS)]
    pair = rows >= LANES
    if pair:
        pos = lax.broadcasted_iota(jnp.int32, (A_DK, LANES), 1)
        pos_mask = [pos < c, pos >= c]
    tail = 8 if group * 8 >= 32 else c
    gn = gn_ref[...]

    def sub_rows(t):
        start = t * rows
        return start if isinstance(start, int) else pl.multiple_of(start, rows)

    def step(te, se, tp, sp):
        if tp is not None:
            rp = pl.ds(sub_rows(tp), rows)
            la = la_ref[rp, :]
            la_hi = la.astype(BF)
            rem = la - la_hi.astype(F32)
            la_mid = rem.astype(BF)
            la_lo = (rem - la_mid.astype(F32)).astype(BF)
            b3 = _dot(tri, jnp.concatenate([la_hi, la_mid, la_lo], axis=1))
        if te is not None:
            row0 = sub_rows(te)
            re = pl.ds(row0, rows)
            states = [s_ref[...]]
            for ci in range(group):
                states.append(dec_ref[se, ci] * states[-1] + inc_ref[se, ci])
            s_ref[...] = states[-1]
            q_bd = [qbd_ref[se, ci] for ci in range(group)]
            scores = [_dot_nt(q_bd[ci], kdec_ref[se, ci * c:(ci + 1) * c, :]) for ci in range(group)]
            o_inter = [_dot(q_bd[ci], states[ci].astype(BF)) for ci in range(group)]
        if tp is not None:
            b = b3[:, :A_QK] + b3[:, A_QK:2 * A_QK] + b3[:, 2 * A_QK:]
            k = k_ref[rp, :]
            q_dec = (q_ref[rp, :] * (A_DK ** -0.5) * jnp.exp(b)).astype(BF)
            kdec_ref[sp] = (k * jnp.exp(-b)).astype(BF)
            b_last = jnp.concatenate(
                [jnp.broadcast_to(b[(ci + 1) * c - 1:(ci + 1) * c, :], (c, A_QK)) for ci in range(group)], axis=0)
            ku_t = (k * jnp.exp(b_last - b)).T.astype(BF)
            tail_t = jnp.concatenate([b[(ci + 1) * c - tail:(ci + 1) * c, :] for ci in range(group)], axis=0).T
            for ci in range(group):
                q_c = q_dec[ci * c:(ci + 1) * c, :]
                qbd_ref[sp, ci] = jnp.concatenate(
                    [jnp.where(head_mask[h], q_c, jnp.zeros_like(q_c)) for h in range(A_HEADS)], axis=0)
                decay = jnp.exp(tail_t[:, (ci + 1) * tail - 1:(ci + 1) * tail])
                dec_ref[sp, ci] = jnp.broadcast_to(decay, (A_QK, A_DV))
        if te is not None:
            v = v_ref[re, :]
            r = r_ref[re, :]
            for ci in range(group):
                cr = slice(ci * c, (ci + 1) * c)
                outs = []
                for h in range(A_HEADS):
                    hs = slice(h * c, (h + 1) * c)
                    hv = slice(h * A_DV, (h + 1) * A_DV)
                    s_h = jnp.where(causal, scores[ci][hs, :], 0.0).astype(BF)
                    o = _dot(s_h, v[cr, hv]) + o_inter[ci][hs, :]
                    outs.append((_rmsnorm(o, gn) * _silu(r[cr, hv])).astype(BF))
                g_ref[pl.ds(row0 + ci * c, c), :] = jnp.concatenate(outs, axis=1)
        if tp is not None:
            v = v_ref[rp, :]
            for ci in range(group):
                incs = []
                for h in range(A_HEADS):
                    hk = slice(h * A_DK, (h + 1) * A_DK)
                    hv = slice(h * A_DV, (h + 1) * A_DV)
                    if pair:
                        p0 = (ci // 2) * LANES
                        lhs = ku_t[hk, p0:p0 + LANES]
                        lhs = jnp.where(pos_mask[ci % 2], lhs, jnp.zeros_like(lhs))
                        incs.append(_dot(lhs, v[p0:p0 + LANES, hv]))
                    else:
                        incs.append(_dot(ku_t[hk, :], v[:, hv]))
                inc_ref[sp, ci] = jnp.concatenate(incs, axis=0)

    s_ref[...] = s0_ref[...]
    step(None, None, 0, 0)
    if n_sub > 1:
        def body(j, carry):
            t = 2 * j
            step(t, 0, t + 1, 1)
            step(t + 1, 1, t + 2, 0)
            return carry

        lax.fori_loop(0, n_sub // 2 - 1, body, 0)
        step(n_sub - 2, 0, n_sub - 1, 1)
        step(n_sub - 1, 1, None, None)
    else:
        step(0, 0, None, None)
    sout_ref[...] = s_ref[...]


def _gla_call(aq, ak, av, la, ar, s0, gn, *, row0, seq, chunk, group, name):
    n_streams = s0.shape[0]
    rows = chunk * group
    n_sub = seq // rows
    assert seq % rows == 0 and (n_sub == 1 or n_sub % 2 == 0) and chunk & (chunk - 1) == 0
    base = row0 // seq
    tok = lambda w: pl.BlockSpec((seq, w), lambda b: (base + b, 0))
    st = pl.BlockSpec((None, A_QK, A_DV), lambda b: (b, 0, 0))
    return pl.pallas_call(
        functools.partial(_gla_kernel, chunk=chunk, group=group, n_sub=n_sub),
        grid=(n_streams,),
        in_specs=[tok(A_QK), tok(A_QK), tok(A_VW), tok(A_QK), tok(A_VW), st,
                  pl.BlockSpec((1, A_DV), lambda b: (0, 0))],
        out_specs=[pl.BlockSpec((seq, A_VW), lambda b: (b, 0)), st],
        out_shape=[jax.ShapeDtypeStruct((n_streams * seq, A_VW), BF),
                   jax.ShapeDtypeStruct((n_streams, A_QK, A_DV), F32)],
        scratch_shapes=[pltpu.VMEM((2, group, A_HEADS * chunk, A_QK), BF), pltpu.VMEM((2, rows, A_QK), BF),
                        pltpu.VMEM((2, group, A_QK, A_DV), F32), pltpu.VMEM((2, group, A_QK, A_DV), F32),
                        pltpu.VMEM((A_QK, A_DV), F32)],
        compiler_params=pltpu.CompilerParams(dimension_semantics=("arbitrary",), vmem_limit_bytes=VMEM_LIMIT),
        name=name,
    )(aq, ak, av, la, ar, s0, gn)


def _bias_block(brow_ref, blk, tq, tk):
    rows = []
    for h in range(4):
        f = jnp.broadcast_to(brow_ref[blk * 4 + h:blk * 4 + h + 1, :], (tq, BIAS_ROW))
        rows.append(pltpu.roll(f, BIAS_ROW - CHUNK, 1, stride=1, stride_axis=0)[:, :tk])
    return jnp.concatenate(rows, axis=0)


def _block_diag_rows(qb):
    lane = lax.broadcasted_iota(jnp.int32, qb.shape, 1)
    return jnp.concatenate(
        [jnp.where((lane >= h * B_HD) & (lane < (h + 1) * B_HD), qb, jnp.zeros_like(qb)) for h in range(4)],
        axis=0)


def _attn_scores(qb, kw, valid):
    s = _dot_nt(_block_diag_rows(qb), kw)
    return s if valid is None else jnp.where(valid, s, -1e30)


def _attn_softmax(s, bias_tail):
    s = jnp.concatenate([s[:, :BIAS_FLAT], s[:, BIAS_FLAT:] + bias_tail], axis=1)
    p = jnp.exp(s - jnp.max(s, axis=-1, keepdims=True))
    l = jnp.sum(p, axis=-1, keepdims=True)
    return p.astype(BF), jnp.broadcast_to(l, (s.shape[0], LANES))


def _attn_values(p, l, vw):
    inv = 1.0 / l
    return _dot(p, vw) * jnp.concatenate([inv, inv], axis=1)


def _attn_norm(o, gain, diag, tq):
    o = jnp.where(diag, o, 0.0)
    ms = jnp.sum(o * o, axis=-1, keepdims=True) * (1.0 / B_HD)
    y = o * lax.rsqrt(ms + EPS) * gain
    return y[0:tq] + y[tq:2 * tq] + y[2 * tq:3 * tq] + y[3 * tq:4 * tq]


def _diag_mask(tq):
    r = lax.broadcasted_iota(jnp.int32, (4 * tq, 4 * B_HD), 0)
    cidx = lax.broadcasted_iota(jnp.int32, (4 * tq, 4 * B_HD), 1)
    m = None
    for h in range(4):
        t = (r >= h * tq) & (r < (h + 1) * tq) & (cidx >= h * B_HD) & (cidx < (h + 1) * B_HD)
        m = t if m is None else (m | t)
    return m


def _attn_prompt_kernel(q_ref, k_ref, v_ref, brow_ref, gain_ref, o_ref,
                        kpad_ref, vpad_ref, bias_ref, s_ref, p_ref, l_ref, acc_ref, *, seq):
    tk = B_LEFT + CHUNK
    n_chunks = seq // CHUNK
    kpad_ref[0:B_LEFT, :] = jnp.zeros((B_LEFT, B_W), BF)
    vpad_ref[0:B_LEFT, :] = jnp.zeros((B_LEFT, B_W), BF)
    kpad_ref[B_LEFT:, :] = k_ref[...]
    vpad_ref[B_LEFT:, :] = v_ref[...]
    for blk in range(2):
        bias_ref[blk] = _bias_block(brow_ref, blk, CHUNK, tk)[:, BIAS_FLAT:]
    gain = gain_ref[...]
    diag = _diag_mask(CHUNK)
    key_pos = lax.broadcasted_iota(jnp.int32, (1, tk), 1)
    blocks = [slice(blk * 4 * B_HD, (blk + 1) * 4 * B_HD) for blk in range(2)]

    def scores_stage(ci, slot, masked):
        r0 = pl.multiple_of(ci * CHUNK, CHUNK)
        valid = (key_pos + r0 >= B_LEFT) if masked else None
        for blk, cs in enumerate(blocks):
            s_ref[slot, blk] = _attn_scores(q_ref[pl.ds(r0, CHUNK), cs], kpad_ref[pl.ds(r0, tk), cs], valid)

    def softmax_stage(slot):
        for blk in range(2):
            p, l = _attn_softmax(s_ref[slot, blk], bias_ref[blk])
            p_ref[slot, blk] = p
            l_ref[slot, blk] = l

    def values_stage(ci, slot):
        r0 = pl.multiple_of(ci * CHUNK, CHUNK)
        for blk, cs in enumerate(blocks):
            acc_ref[slot, blk] = _attn_values(p_ref[slot, blk], l_ref[slot, blk], vpad_ref[pl.ds(r0, tk), cs])

    def norm_stage(ci, slot):
        r0 = pl.multiple_of(ci * CHUNK, CHUNK)
        for blk, cs in enumerate(blocks):
            o_ref[pl.ds(r0, CHUNK), cs] = _attn_norm(acc_ref[slot, blk], gain, diag, CHUNK).astype(BF)

    def static_step(i, masked):
        if i < n_chunks:
            scores_stage(i, i % 2, masked)
        if 0 <= i - 1 < n_chunks:
            softmax_stage((i - 1) % 2)
        if 0 <= i - 2 < n_chunks:
            values_stage(i - 2, i % 2)
        if 0 <= i - 3 < n_chunks:
            norm_stage(i - 3, (i - 1) % 2)

    def pair(masked):
        def body(j, carry):
            for d in range(2):
                i = 2 * j + d
                scores_stage(i, d, masked)
                softmax_stage(1 - d)
                values_stage(i - 2, d)
                norm_stage(i - 3, 1 - d)
            return carry
        return body

    n_masked = B_LEFT // CHUNK
    for i in range(4):
        static_step(i, True)
    lax.fori_loop(2, n_masked // 2, pair(True), 0)
    lax.fori_loop(n_masked // 2, n_chunks // 2, pair(False), 0)
    for i in range(n_chunks, n_chunks + 3):
        static_step(i, False)


def _attn_prompt_call(bq, bk, bv, brow, gain, *, n_streams, seq):
    tok = pl.BlockSpec((seq, B_W), lambda b: (b, 0))
    return pl.pallas_call(
        functools.partial(_attn_prompt_kernel, seq=seq),
        grid=(n_streams,),
        in_specs=[tok, tok, tok, pl.BlockSpec(brow.shape, lambda b: (0, 0)),
                  pl.BlockSpec(gain.shape, lambda b: (0, 0))],
        out_specs=tok,
        out_shape=jax.ShapeDtypeStruct((n_streams * seq, B_W), BF),
        scratch_shapes=[pltpu.VMEM((B_LEFT + seq, B_W), BF), pltpu.VMEM((B_LEFT + seq, B_W), BF),
                        pltpu.VMEM((2, 4 * CHUNK, B_LEFT + CHUNK - BIAS_FLAT), F32),
                        pltpu.VMEM((2, 2, 4 * CHUNK, B_LEFT + CHUNK), F32),
                        pltpu.VMEM((2, 2, 4 * CHUNK, B_LEFT + CHUNK), BF),
                        pltpu.VMEM((2, 2, 4 * CHUNK, LANES), F32),
                        pltpu.VMEM((2, 2, 4 * CHUNK, 4 * B_HD), F32)],
        compiler_params=pltpu.CompilerParams(dimension_semantics=("arbitrary",), vmem_limit_bytes=VMEM_LIMIT),
        name="attn_prompt",
    )(bq, bk, bv, brow, gain)


def _attn_sample_kernel(q_ref, k_ref, v_ref, ck_ref, cv_ref, brow_ref, gain_ref, o_ref, *, tq):
    w = ck_ref.shape[0]
    gain = gain_ref[...]
    diag = _diag_mask(tq)
    for blk in range(2):
        cs = slice(blk * 4 * B_HD, (blk + 1) * 4 * B_HD)
        kw = jnp.concatenate([ck_ref[:, cs].astype(BF), k_ref[:, cs]], axis=0)
        vw = jnp.concatenate([cv_ref[:, cs].astype(BF), v_ref[:, cs]], axis=0)
        bias_tail = _bias_block(brow_ref, blk, tq, w + tq)[:, BIAS_FLAT:]
        p, l = _attn_softmax(_attn_scores(q_ref[:, cs], kw, None), bias_tail)
        out = _attn_norm(_attn_values(p, l, vw), gain, diag, tq)
        o_ref[:, cs] = out.astype(BF)


def _attn_sample_call(bq, bk, bv, ck, cv, brow, gain, *, row0, tq):
    n_streams, w, _ = ck.shape
    base = row0 // tq
    tok = pl.BlockSpec((tq, B_W), lambda b: (base + b, 0))
    cache = pl.BlockSpec((None, w, B_W), lambda b: (b, 0, 0))
    return pl.pallas_call(
        functools.partial(_attn_sample_kernel, tq=tq),
        grid=(n_streams,),
        in_specs=[tok, tok, tok, cache, cache, pl.BlockSpec(brow.shape, lambda b: (0, 0)),
                  pl.BlockSpec(gain.shape, lambda b: (0, 0))],
        out_specs=pl.BlockSpec((tq, B_W), lambda b: (b, 0)),
        out_shape=jax.ShapeDtypeStruct((n_streams * tq, B_W), BF),
        compiler_params=pltpu.CompilerParams(dimension_semantics=("arbitrary",), vmem_limit_bytes=VMEM_LIMIT),
        name="attn_sample",
    )(bq, bk, bv, ck, cv, brow, gain)


def kernel(x_prompt, x_sample, state_gla, cache_k, cache_v, norm_ffn1, w_ffn1_gate, w_ffn1_up, w_ffn1_down,
           norm_mix, w_in, w_alpha, b_alpha, gla_norm, attn_bias, attn_norm, w_out, norm_ffn2, w_ffn2_gate,
           w_ffn2_up, w_ffn2_down, final_norm):
    batch, seq, _ = x_prompt.shape
    dec_batch, dec_seq, _ = x_sample.shape
    n_prompt = batch * seq
    n_sample = dec_batch * dec_seq
    past = cache_k.shape[2]
    assert seq % TM == 0 and n_sample % TM == 0 and TM == B_LEFT and past == B_LEFT
    assert dec_seq <= CHUNK
    assert seq % (2 * CHUNK) == 0 and seq >= B_LEFT + 2 * CHUNK

    x = (x_prompt.reshape(n_prompt, D_MODEL), x_sample.reshape(n_sample, D_MODEL))
    row = lambda v: v.reshape(1, -1).astype(F32)
    sizes = dict(n_prompt=n_prompt, n_sample=n_sample)

    o_gate = 2 * A_QK + A_VW
    o_r = o_gate + A_RANK
    o_b = o_r + A_VW

    gla_p, gla_s, k_p, v_p, k_s, v_s = [], [], [], [], [], []
    zero_state = jnp.zeros((batch, A_QK, A_DV), F32)
    for l in range(DEPTH):
        bf = lambda w: w.astype(BF)
        x = _ffn_call(x, row(norm_ffn1[l]), bf(w_ffn1_gate[l]), bf(w_ffn1_up[l]), bf(w_ffn1_down[l]), **sizes)

        wl = w_in[l]
        wgate = jnp.pad(wl[:, o_gate:o_r], ((0, 0), (0, LANES - A_RANK)))
        walpha = jnp.pad(w_alpha[l], ((0, LANES - A_RANK), (0, 0)))
        aq, ak, av, la, ar, bq, bk, bv, kp, vp, ks, vs = _inproj_call(
            x, row(norm_mix[l]), bf(wl[:, :o_gate]), bf(wgate), bf(wl[:, o_r:o_b]), bf(wl[:, o_b:]),
            bf(walpha), row(b_alpha[l]), n_prompt=n_prompt, seq=seq)

        gn = row(gla_norm[l])
        g_p, sp = _gla_call(aq, ak, av, la, ar, zero_state, gn, row0=0, seq=seq, chunk=CHUNK,
                            group=GLA_GROUP, name="gla_prompt")
        g_s, ss = _gla_call(aq, ak, av, la, ar, state_gla[l].reshape(dec_batch, A_QK, A_DV), gn,
                            row0=n_prompt, seq=dec_seq, chunk=dec_seq, group=1, name="gla_sample")

        brow = jnp.pad(attn_bias[l][:, ::-1],
                       ((0, 0), (B_LEFT - CHUNK, BIAS_ROW - (B_LEFT - CHUNK) - (2 * REL_CLIP + 1))), mode="edge")
        brow = brow - brow[:, :1]
        gain = jnp.tile(row(attn_norm[l]), (1, 4))
        a_p = _attn_prompt_call(bq, bk, bv, brow, gain, n_streams=batch, seq=seq)
        a_s = _attn_sample_call(bq, bk, bv, cache_k[l].reshape(dec_batch, past, B_W),
                                cache_v[l].reshape(dec_batch, past, B_W), brow, gain, row0=n_prompt, tq=dec_seq)

        wo = bf(w_out[l])
        x = _ffn_call(x, row(norm_ffn2[l]), bf(w_ffn2_gate[l]), bf(w_ffn2_up[l]), bf(w_ffn2_down[l]),
                      mix=((g_p, g_s), (a_p, a_s), wo[:A_VW], wo[A_VW:]),
                      final_norm=row(final_norm) if l == DEPTH - 1 else None, **sizes)
        gla_p.append(sp.reshape(batch, A_HEADS, A_DK, A_DV))
        gla_s.append(ss.reshape(dec_batch, A_HEADS, A_DK, A_DV))
        k_p.append(kp.reshape(batch, B_LEFT, B_HEADS, B_HD))
        v_p.append(vp.reshape(batch, B_LEFT, B_HEADS, B_HD))
        k_s.append(ks.reshape(dec_batch, dec_seq, B_HEADS, B_HD))
        v_s.append(vs.reshape(dec_batch, dec_seq, B_HEADS, B_HD))

    y_prompt, y_sample = x
    return (y_prompt.reshape(batch, seq, D_MODEL), y_sample.reshape(dec_batch, dec_seq, D_MODEL),
            jnp.stack(gla_p), jnp.stack(k_p), jnp.stack(v_p), jnp.stack(gla_s), jnp.stack(k_s), jnp.stack(v_s))
```

```python
import functools

import jax
import jax.numpy as jnp
from jax import lax
from jax.experimental import pallas as pl
from jax.experimental.pallas import tpu as pltpu

F32 = jnp.float32
BF = jnp.bfloat16

D_MODEL = 1024
D_FF = 2816
DEPTH = 2
CHUNK = 64
A_HEADS = 4
A_DK = 64
A_DV = 128
A_QK = A_HEADS * A_DK
A_VW = A_HEADS * A_DV
A_RANK = 16
A_TAU = 16.0
B_HEADS = 8
B_HD = 64
B_W = B_HEADS * B_HD
B_LEFT = 8 * CHUNK
REL_CLIP = 128
EPS = 1e-6

TM = 512
FF_CHUNK = 256
GLA_GROUP = 4
LANES = 128
BIAS_ROW = 768
BIAS_FLAT = B_LEFT - REL_CLIP
VMEM_LIMIT = 56 * 1024 * 1024
LOG2E = 1.4426950408889634
Q_SCALE = B_HD ** -0.5 * LOG2E


def _rmsnorm(x, g):
    ms = jnp.mean(x * x, axis=-1, keepdims=True)
    return x * lax.rsqrt(ms + EPS) * g


def _silu(x):
    return x * jax.nn.sigmoid(x)


def _dot(a, b):
    return jnp.dot(a, b, preferred_element_type=F32)


def _dot_nt(a, b):
    return lax.dot_general(a, b, (((1,), (1,)), ((), ())), preferred_element_type=F32)


def _const_spec(shape):
    return pl.BlockSpec(shape, lambda *_: (0,) * len(shape), pipeline_mode=pl.Buffered(1))


def _ffn_kernel(*refs, first, mix, final, prompt_tiles):
    refs = list(refs)
    i = pl.program_id(0)
    if first:
        xp_ref, xs_ref = refs[:2]
        refs = refs[2:]
        x = jnp.where(i < prompt_tiles, xp_ref[...], xs_ref[...])
    else:
        x = refs.pop(0)[...]
    if mix:
        gp_ref, gs_ref, ap_ref, as_ref, wog_ref, woa_ref = refs[:6]
        refs = refs[6:]
        g = jnp.where(i < prompt_tiles, gp_ref[...], gs_ref[...])
        a = jnp.where(i < prompt_tiles, ap_ref[...], as_ref[...])
        x = x + _dot(g, wog_ref[...]) + _dot(a, woa_ref[...])
    n_ref, wg_ref, wu_ref, wd_ref = refs[:4]
    refs = refs[4:]
    if final:
        fn_ref = refs.pop(0)
    act_ref = refs.pop()

    h = _rmsnorm(x, n_ref[...]).astype(BF)
    for c in range(D_FF // FF_CHUNK):
        sl = slice(c * FF_CHUNK, (c + 1) * FF_CHUNK)
        g = _dot(h, wg_ref[:, sl])
        u = _dot(h, wu_ref[:, sl])
        act_ref[:, sl] = (_silu(g) * u).astype(BF)
    y = x + 0.5 * _dot(act_ref[...], wd_ref[...])
    if not final:
        refs[0][...] = y
        return
    y = _rmsnorm(y, fn_ref[...])
    yp_ref, ys_ref = refs

    @pl.when(i < prompt_tiles)
    def _():
        yp_ref[...] = y

    @pl.when(i >= prompt_tiles)
    def _():
        ys_ref[...] = y


def _split_specs(width, prompt_tiles):
    return [pl.BlockSpec((TM, width), lambda i: (jnp.minimum(i, prompt_tiles - 1), 0)),
            pl.BlockSpec((TM, width), lambda i: (jnp.maximum(i - prompt_tiles, 0), 0))]


def _ffn_call(x, norm, wg, wu, wd, *, n_prompt, n_sample, mix=None, final_norm=None):
    n = n_prompt + n_sample
    prompt_tiles = n_prompt // TM
    first = isinstance(x, tuple)
    tok = lambda w: pl.BlockSpec((TM, w), lambda i: (i, 0))
    args = list(x) if first else [x]
    specs = _split_specs(D_MODEL, prompt_tiles) if first else [tok(D_MODEL)]
    if mix is not None:
        g, a, wog, woa = mix
        args += [*g, *a, wog, woa]
        specs += _split_specs(A_VW, prompt_tiles) + _split_specs(B_W, prompt_tiles)
        specs += [_const_spec((A_VW, D_MODEL)), _const_spec((B_W, D_MODEL))]
    args += [norm, wg, wu, wd]
    specs += [_const_spec((1, D_MODEL)), _const_spec((D_MODEL, D_FF)), _const_spec((D_MODEL, D_FF)),
              _const_spec((D_FF, D_MODEL))]
    if final_norm is not None:
        args.append(final_norm)
        specs.append(_const_spec((1, D_MODEL)))
        out_specs = _split_specs(D_MODEL, prompt_tiles)
        out_shape = [jax.ShapeDtypeStruct((n_prompt, D_MODEL), F32), jax.ShapeDtypeStruct((n_sample, D_MODEL), F32)]
    else:
        out_specs = tok(D_MODEL)
        out_shape = jax.ShapeDtypeStruct((n, D_MODEL), F32)
    return pl.pallas_call(
        functools.partial(_ffn_kernel, first=first, mix=mix is not None, final=final_norm is not None,
                          prompt_tiles=prompt_tiles),
        grid=(n // TM,),
        in_specs=specs,
        out_specs=out_specs,
        out_shape=out_shape,
        scratch_shapes=[pltpu.VMEM((TM, D_FF), BF)],
        compiler_params=pltpu.CompilerParams(dimension_semantics=("arbitrary",), vmem_limit_bytes=VMEM_LIMIT),
        name="ffn_mix" if mix is not None else "ffn",
    )(*args)


def _inproj_kernel(x_ref, n_ref, wa_ref, wgate_ref, wr_ref, wb_ref, walpha_ref, balpha_ref,
                   aq_ref, ak_ref, av_ref, la_ref, ar_ref, bq_ref, bk_ref, bv_ref):
    h = _rmsnorm(x_ref[...], n_ref[...]).astype(BF)
    ag = _dot(h, wgate_ref[...])
    za = _dot(h, wa_ref[...])
    pre = _dot(ag.astype(BF), walpha_ref[...]) + balpha_ref[...]
    aq_ref[...] = za[:, :A_QK]
    ak_ref[...] = za[:, A_QK:2 * A_QK]
    av_ref[...] = za[:, 2 * A_QK:].astype(BF)
    ar_ref[...] = _dot(h, wr_ref[...])
    log_sig = jnp.minimum(pre, 0.0) - jnp.log1p(jnp.exp(-jnp.abs(pre)))
    la_ref[...] = log_sig * (1.0 / A_TAU)
    zb = _dot(h, wb_ref[...])
    bq_ref[...] = (zb[:, :B_W] * Q_SCALE).astype(BF)
    bk_ref[...] = zb[:, B_W:2 * B_W].astype(BF)
    bv_ref[...] = zb[:, 2 * B_W:].astype(BF)


def _inproj_call(x, norm, wa, wgate, wr, wb, walpha, balpha):
    n = x.shape[0]
    tok = lambda w: pl.BlockSpec((TM, w), lambda i: (i, 0))
    outs = [
        (A_QK, F32), (A_QK, F32), (A_VW, BF), (A_QK, F32), (A_VW, F32), (B_W, BF), (B_W, BF), (B_W, BF)]
    return pl.pallas_call(
        _inproj_kernel,
        grid=(n // TM,),
        in_specs=[tok(D_MODEL), _const_spec((1, D_MODEL)), _const_spec(wa.shape), _const_spec(wgate.shape),
                  _const_spec(wr.shape), _const_spec(wb.shape), _const_spec(walpha.shape),
                  _const_spec(balpha.shape)],
        out_specs=[tok(w) for w, _ in outs],
        out_shape=[jax.ShapeDtypeStruct((n, w), dt) for w, dt in outs],
        compiler_params=pltpu.CompilerParams(dimension_semantics=("arbitrary",), vmem_limit_bytes=VMEM_LIMIT),
        name="inproj",
    )(x, norm, wa, wgate, wr, wb, walpha, balpha)


def _kv_tails_kernel(*refs, prompt_tail_tiles):
    x_refs = refs[:DEPTH]
    n_ref, w_ref, kp_ref, vp_ref, ks_ref, vs_ref = refs[DEPTH:]
    layer = pl.program_id(0)
    t = pl.program_id(1)
    x = x_refs[0][...]
    for l in range(1, DEPTH):
        x = jnp.where(layer == l, x_refs[l][...], x)
    kv = _dot(_rmsnorm(x, n_ref[...]).astype(BF), w_ref[...])

    def put(k_out, v_out):
        for hd in range(B_HEADS):
            k_out[pl.ds(hd, TM, stride=B_HEADS), :] = kv[:, hd * B_HD:(hd + 1) * B_HD]
            v_out[pl.ds(hd, TM, stride=B_HEADS), :] = kv[:, B_W + hd * B_HD:B_W + (hd + 1) * B_HD]

    @pl.when(t < prompt_tail_tiles)
    def _():
        put(kp_ref, vp_ref)

    @pl.when(t >= prompt_tail_tiles)
    def _():
        put(ks_ref, vs_ref)


def _kv_tails_call(xs, norms, w_kv, *, n_prompt, seq):
    n = xs[0].shape[0]
    tiles_per_stream = seq // TM
    prompt_tiles = n_prompt // TM
    n_streams = n_prompt // seq
    sample_tiles = n // TM - prompt_tiles
    tail_tok = pl.BlockSpec(
        (TM, D_MODEL),
        lambda l, t: (jnp.where(t < n_streams, t * tiles_per_stream + tiles_per_stream - 1,
                                t - n_streams + prompt_tiles), 0))
    rows = TM * B_HEADS
    p_out = pl.BlockSpec((None, rows, B_HD), lambda l, t: (l, jnp.minimum(t, n_streams - 1), 0))
    s_out = pl.BlockSpec((None, rows, B_HD), lambda l, t: (l, jnp.maximum(t - n_streams, 0), 0))
    p_shape = jax.ShapeDtypeStruct((DEPTH, n_streams * rows, B_HD), F32)
    s_shape = jax.ShapeDtypeStruct((DEPTH, sample_tiles * rows, B_HD), F32)
    return pl.pallas_call(
        functools.partial(_kv_tails_kernel, prompt_tail_tiles=n_streams),
        grid=(DEPTH, n_streams + sample_tiles),
        in_specs=[tail_tok] * DEPTH + [pl.BlockSpec((None, 1, D_MODEL), lambda l, t: (l, 0, 0)),
                                       pl.BlockSpec((None, D_MODEL, 2 * B_W), lambda l, t: (l, 0, 0))],
        out_specs=[p_out, p_out, s_out, s_out],
        out_shape=[p_shape, p_shape, s_shape, s_shape],
        compiler_params=pltpu.CompilerParams(dimension_semantics=("arbitrary", "arbitrary"),
                                             vmem_limit_bytes=VMEM_LIMIT),
        name="kv_tails",
    )(*xs, norms, w_kv)


def _gla_kernel(q_ref, k_ref, v_ref, la_ref, r_ref, s0_ref, gn_ref, g_ref, sout_ref,
                qbd_ref, kdec_ref, inc_ref, dec_ref, s_ref, *, chunk, group, n_sub):
    c = chunk
    rows = group * c
    r_i = lax.broadcasted_iota(jnp.int32, (rows, rows), 0)
    c_i = lax.broadcasted_iota(jnp.int32, (rows, rows), 1)
    same_chunk = (r_i & -c) == (c_i & -c)
    tri = jnp.where(same_chunk & (r_i >= c_i), 1.0, 0.0).astype(BF)
    causal = lax.broadcasted_iota(jnp.int32, (c, c), 0) >= lax.broadcasted_iota(jnp.int32, (c, c), 1)
    lane = lax.broadcasted_iota(jnp.int32, (c, A_QK), 1)
    head_mask = [(lane >= h * A_DK) & (lane < (h + 1) * A_DK) for h in range(A_HEADS)]
    pair = rows >= LANES
    if pair:
        pos = lax.broadcasted_iota(jnp.int32, (A_DK, LANES), 1)
        pos_mask = [pos < c, pos >= c]
    tail = 8 if group * 8 >= 32 else c
    gn = gn_ref[...]

    def sub_rows(t):
        start = t * rows
        return start if isinstance(start, int) else pl.multiple_of(start, rows)

    def step(te, se, tp, sp):
        if tp is not None:
            rp = pl.ds(sub_rows(tp), rows)
            la = la_ref[rp, :]
            la_hi = la.astype(BF)
            rem = la - la_hi.astype(F32)
            la_mid = rem.astype(BF)
            la_lo = (rem - la_mid.astype(F32)).astype(BF)
            b3 = _dot(tri, jnp.concatenate([la_hi, la_mid, la_lo], axis=1))
        if te is not None:
            row0 = sub_rows(te)
            re = pl.ds(row0, rows)
            states = [s_ref[...]]
            for ci in range(group):
                states.append(dec_ref[se, ci] * states[-1] + inc_ref[se, ci])
            s_ref[...] = states[-1]
            q_bd = [qbd_ref[se, ci] for ci in range(group)]
            scores = [_dot_nt(q_bd[ci], kdec_ref[se, ci * c:(ci + 1) * c, :]) for ci in range(group)]
            o_inter = [_dot(q_bd[ci], states[ci].astype(BF)) for ci in range(group)]
        if tp is not None:
            b = b3[:, :A_QK] + b3[:, A_QK:2 * A_QK] + b3[:, 2 * A_QK:]
            k = k_ref[rp, :]
            q_dec = (q_ref[rp, :] * (A_DK ** -0.5) * jnp.exp(b)).astype(BF)
            kdec_ref[sp] = (k * jnp.exp(-b)).astype(BF)
            b_last = jnp.concatenate(
                [jnp.broadcast_to(b[(ci + 1) * c - 1:(ci + 1) * c, :], (c, A_QK)) for ci in range(group)], axis=0)
            ku_t = (k * jnp.exp(b_last - b)).T.astype(BF)
            tail_t = jnp.concatenate([b[(ci + 1) * c - tail:(ci + 1) * c, :] for ci in range(group)], axis=0).T
            for ci in range(group):
                q_c = q_dec[ci * c:(ci + 1) * c, :]
                qbd_ref[sp, ci] = jnp.concatenate(
                    [jnp.where(head_mask[h], q_c, jnp.zeros_like(q_c)) for h in range(A_HEADS)], axis=0)
                decay = jnp.exp(tail_t[:, (ci + 1) * tail - 1:(ci + 1) * tail])
                dec_ref[sp, ci] = jnp.broadcast_to(decay, (A_QK, A_DV))
        if te is not None:
            v = v_ref[re, :]
            r = r_ref[re, :]
            for ci in range(group):
                cr = slice(ci * c, (ci + 1) * c)
                outs = []
                for h in range(A_HEADS):
                    hs = slice(h * c, (h + 1) * c)
                    hv = slice(h * A_DV, (h + 1) * A_DV)
                    s_h = jnp.where(causal, scores[ci][hs, :], 0.0).astype(BF)
                    o = _dot(s_h, v[cr, hv]) + o_inter[ci][hs, :]
                    outs.append((_rmsnorm(o, gn) * _silu(r[cr, hv])).astype(BF))
                g_ref[pl.ds(row0 + ci * c, c), :] = jnp.concatenate(outs, axis=1)
        if tp is not None:
            v = v_ref[rp, :]
            for ci in range(group):
                incs = []
                for h in range(A_HEADS):
                    hk = slice(h * A_DK, (h + 1) * A_DK)
                    hv = slice(h * A_DV, (h + 1) * A_DV)
                    if pair:
                        p0 = (ci // 2) * LANES
                        lhs = ku_t[hk, p0:p0 + LANES]
                        lhs = jnp.where(pos_mask[ci % 2], lhs, jnp.zeros_like(lhs))
                        incs.append(_dot(lhs, v[p0:p0 + LANES, hv]))
                    else:
                        incs.append(_dot(ku_t[hk, :], v[:, hv]))
                inc_ref[sp, ci] = jnp.concatenate(incs, axis=0)

    s_ref[...] = s0_ref[...]
    step(None, None, 0, 0)
    if n_sub > 1:
        def body(j, carry):
            t = 2 * j
            step(t, 0, t + 1, 1)
            step(t + 1, 1, t + 2, 0)
            return carry

        lax.fori_loop(0, n_sub // 2 - 1, body, 0)
        step(n_sub - 2, 0, n_sub - 1, 1)
        step(n_sub - 1, 1, None, None)
    else:
        step(0, 0, None, None)
    sout_ref[...] = s_ref[...]


def _gla_call(aq, ak, av, la, ar, s0, gn, *, row0, seq, chunk, group, name):
    n_streams = s0.shape[0]
    rows = chunk * group
    n_sub = seq // rows
    assert seq % rows == 0 and (n_sub == 1 or n_sub % 2 == 0) and chunk & (chunk - 1) == 0
    base = row0 // seq
    tok = lambda w: pl.BlockSpec((seq, w), lambda b: (base + b, 0))
    st = pl.BlockSpec((None, A_QK, A_DV), lambda b: (b, 0, 0))
    return pl.pallas_call(
        functools.partial(_gla_kernel, chunk=chunk, group=group, n_sub=n_sub),
        grid=(n_streams,),
        in_specs=[tok(A_QK), tok(A_QK), tok(A_VW), tok(A_QK), tok(A_VW), st,
                  pl.BlockSpec((1, A_DV), lambda b: (0, 0))],
        out_specs=[pl.BlockSpec((seq, A_VW), lambda b: (b, 0)), st],
        out_shape=[jax.ShapeDtypeStruct((n_streams * seq, A_VW), BF),
                   jax.ShapeDtypeStruct((n_streams, A_QK, A_DV), F32)],
        scratch_shapes=[pltpu.VMEM((2, group, A_HEADS * chunk, A_QK), BF), pltpu.VMEM((2, rows, A_QK), BF),
                        pltpu.VMEM((2, group, A_QK, A_DV), F32), pltpu.VMEM((2, group, A_QK, A_DV), F32),
                        pltpu.VMEM((A_QK, A_DV), F32)],
        compiler_params=pltpu.CompilerParams(dimension_semantics=("arbitrary",), vmem_limit_bytes=VMEM_LIMIT),
        name=name,
    )(aq, ak, av, la, ar, s0, gn)


def _bias_block(brow_ref, blk, tq, tk):
    rows = []
    for h in range(4):
        f = jnp.broadcast_to(brow_ref[blk * 4 + h:blk * 4 + h + 1, :], (tq, BIAS_ROW))
        rows.append(pltpu.roll(f, BIAS_ROW - CHUNK, 1, stride=1, stride_axis=0)[:, :tk])
    return jnp.concatenate(rows, axis=0)


def _block_diag_rows(qb):
    lane = lax.broadcasted_iota(jnp.int32, qb.shape, 1)
    return jnp.concatenate(
        [jnp.where((lane >= h * B_HD) & (lane < (h + 1) * B_HD), qb, jnp.zeros_like(qb)) for h in range(4)],
        axis=0)


def _attn_scores(qb, kw, valid):
    s = _dot_nt(_block_diag_rows(qb), kw)
    return s if valid is None else jnp.where(valid, s, -1e30)


def _attn_softmax(s, bias_tail):
    s = jnp.concatenate([s[:, :BIAS_FLAT], s[:, BIAS_FLAT:] + bias_tail], axis=1)
    p = jnp.exp2(s - jnp.max(s, axis=-1, keepdims=True))
    l = jnp.sum(p, axis=-1, keepdims=True)
    return p.astype(BF), jnp.broadcast_to(l, (s.shape[0], LANES))


def _attn_norm(o, l, gain, diag, tq):
    o = jnp.where(diag, o, 0.0)
    inv = 1.0 / l
    ms = jnp.sum(o * o, axis=-1, keepdims=True) * (inv * inv * (1.0 / B_HD))
    f = inv * lax.rsqrt(ms + EPS)
    y = o * jnp.concatenate([f, f], axis=1)
    return (y[0:tq] + y[tq:2 * tq] + y[2 * tq:3 * tq] + y[3 * tq:4 * tq]) * gain


def _diag_mask(tq):
    r = lax.broadcasted_iota(jnp.int32, (4 * tq, 4 * B_HD), 0)
    cidx = lax.broadcasted_iota(jnp.int32, (4 * tq, 4 * B_HD), 1)
    m = None
    for h in range(4):
        t = (r >= h * tq) & (r < (h + 1) * tq) & (cidx >= h * B_HD) & (cidx < (h + 1) * B_HD)
        m = t if m is None else (m | t)
    return m


def _attn_prompt_kernel(q_ref, k_ref, v_ref, brow_ref, gain_ref, o_ref,
                        kpad_ref, vpad_ref, bias_ref, s_ref, p_ref, l_ref, acc_ref, *, seq):
    tk = B_LEFT + CHUNK
    n_chunks = seq // CHUNK
    kpad_ref[0:B_LEFT, :] = jnp.zeros((B_LEFT, B_W), BF)
    vpad_ref[0:B_LEFT, :] = jnp.zeros((B_LEFT, B_W), BF)
    kpad_ref[B_LEFT:, :] = k_ref[...]
    vpad_ref[B_LEFT:, :] = v_ref[...]
    for blk in range(2):
        bias_ref[blk] = _bias_block(brow_ref, blk, CHUNK, tk)[:, BIAS_FLAT:]
    gain = gain_ref[...]
    diag = _diag_mask(CHUNK)
    key_pos = lax.broadcasted_iota(jnp.int32, (1, tk), 1)
    blocks = [slice(blk * 4 * B_HD, (blk + 1) * 4 * B_HD) for blk in range(2)]

    def scores_stage(ci, slot, masked):
        r0 = pl.multiple_of(ci * CHUNK, CHUNK)
        valid = (key_pos + r0 >= B_LEFT) if masked else None
        for blk, cs in enumerate(blocks):
            s_ref[slot, blk] = _attn_scores(q_ref[pl.ds(r0, CHUNK), cs], kpad_ref[pl.ds(r0, tk), cs], valid)

    def softmax_stage(slot):
        for blk in range(2):
            p, l = _attn_softmax(s_ref[slot, blk], bias_ref[blk])
            p_ref[slot, blk] = p
            l_ref[slot, blk] = l

    def values_stage(ci, slot):
        r0 = pl.multiple_of(ci * CHUNK, CHUNK)
        for blk, cs in enumerate(blocks):
            acc_ref[slot, blk] = _dot(p_ref[slot, blk], vpad_ref[pl.ds(r0, tk), cs])

    def norm_stage(ci, slot):
        r0 = pl.multiple_of(ci * CHUNK, CHUNK)
        for blk, cs in enumerate(blocks):
            o_ref[pl.ds(r0, CHUNK), cs] = _attn_norm(
                acc_ref[slot, blk], l_ref[slot, blk], gain, diag, CHUNK).astype(BF)

    def static_step(i, masked):
        if i < n_chunks:
            scores_stage(i, i % 2, masked)
        if 0 <= i - 3 < n_chunks:
            norm_stage(i - 3, (i - 1) % 2)
        if 0 <= i - 1 < n_chunks:
            softmax_stage((i - 1) % 2)
        if 0 <= i - 2 < n_chunks:
            values_stage(i - 2, i % 2)

    def pair(masked):
        def body(j, carry):
            for d in range(2):
                i = 2 * j + d
                scores_stage(i, d, masked)
                norm_stage(i - 3, 1 - d)
                softmax_stage(1 - d)
                values_stage(i - 2, d)
            return carry
        return body

    n_masked = B_LEFT // CHUNK
    for i in range(4):
        static_step(i, True)
    lax.fori_loop(2, n_masked // 2, pair(True), 0)
    lax.fori_loop(n_masked // 2, n_chunks // 2, pair(False), 0)
    for i in range(n_chunks, n_chunks + 3):
        static_step(i, False)


def _attn_prompt_call(bq, bk, bv, brow, gain, *, n_streams, seq):
    tok = pl.BlockSpec((seq, B_W), lambda b: (b, 0))
    return pl.pallas_call(
        functools.partial(_attn_prompt_kernel, seq=seq),
        grid=(n_streams,),
        in_specs=[tok, tok, tok, pl.BlockSpec(brow.shape, lambda b: (0, 0)),
                  pl.BlockSpec(gain.shape, lambda b: (0, 0))],
        out_specs=tok,
        out_shape=jax.ShapeDtypeStruct((n_streams * seq, B_W), BF),
        scratch_shapes=[pltpu.VMEM((B_LEFT + seq, B_W), BF), pltpu.VMEM((B_LEFT + seq, B_W), BF),
                        pltpu.VMEM((2, 4 * CHUNK, B_LEFT + CHUNK - BIAS_FLAT), F32),
                        pltpu.VMEM((2, 2, 4 * CHUNK, B_LEFT + CHUNK), F32),
                        pltpu.VMEM((2, 2, 4 * CHUNK, B_LEFT + CHUNK), BF),
                        pltpu.VMEM((2, 2, 4 * CHUNK, LANES), F32),
                        pltpu.VMEM((2, 2, 4 * CHUNK, 4 * B_HD), F32)],
        compiler_params=pltpu.CompilerParams(dimension_semantics=("arbitrary",), vmem_limit_bytes=VMEM_LIMIT),
        name="attn_prompt",
    )(bq, bk, bv, brow, gain)


def _attn_sample_kernel(q_ref, k_ref, v_ref, ck_ref, cv_ref, brow_ref, gain_ref, o_ref, *, tq):
    w = ck_ref.shape[0]
    gain = gain_ref[...]
    diag = _diag_mask(tq)
    for blk in range(2):
        cs = slice(blk * 4 * B_HD, (blk + 1) * 4 * B_HD)
        kw = jnp.concatenate([ck_ref[:, cs].astype(BF), k_ref[:, cs]], axis=0)
        vw = jnp.concatenate([cv_ref[:, cs].astype(BF), v_ref[:, cs]], axis=0)
        bias_tail = _bias_block(brow_ref, blk, tq, w + tq)[:, BIAS_FLAT:]
        p, l = _attn_softmax(_attn_scores(q_ref[:, cs], kw, None), bias_tail)
        out = _attn_norm(_dot(p, vw), l, gain, diag, tq)
        o_ref[:, cs] = out.astype(BF)


def _attn_sample_call(bq, bk, bv, ck, cv, brow, gain, *, row0, tq):
    n_streams, w, _ = ck.shape
    base = row0 // tq
    tok = pl.BlockSpec((tq, B_W), lambda b: (base + b, 0))
    cache = pl.BlockSpec((None, w, B_W), lambda b: (b, 0, 0))
    return pl.pallas_call(
        functools.partial(_attn_sample_kernel, tq=tq),
        grid=(n_streams,),
        in_specs=[tok, tok, tok, cache, cache, pl.BlockSpec(brow.shape, lambda b: (0, 0)),
                  pl.BlockSpec(gain.shape, lambda b: (0, 0))],
        out_specs=pl.BlockSpec((tq, B_W), lambda b: (b, 0)),
        out_shape=jax.ShapeDtypeStruct((n_streams * tq, B_W), BF),
        compiler_params=pltpu.CompilerParams(dimension_semantics=("arbitrary",), vmem_limit_bytes=VMEM_LIMIT),
        name="attn_sample",
    )(bq, bk, bv, ck, cv, brow, gain)


def kernel(x_prompt, x_sample, state_gla, cache_k, cache_v, norm_ffn1, w_ffn1_gate, w_ffn1_up, w_ffn1_down,
           norm_mix, w_in, w_alpha, b_alpha, gla_norm, attn_bias, attn_norm, w_out, norm_ffn2, w_ffn2_gate,
           w_ffn2_up, w_ffn2_down, final_norm):
    batch, seq, _ = x_prompt.shape
    dec_batch, dec_seq, _ = x_sample.shape
    n_prompt = batch * seq
    n_sample = dec_batch * dec_seq
    past = cache_k.shape[2]
    assert seq % TM == 0 and n_sample % TM == 0 and TM == B_LEFT and past == B_LEFT
    assert dec_seq <= CHUNK
    assert seq % (2 * CHUNK) == 0 and seq >= B_LEFT + 2 * CHUNK

    x = (x_prompt.reshape(n_prompt, D_MODEL), x_sample.reshape(n_sample, D_MODEL))
    row = lambda v: v.reshape(1, -1).astype(F32)
    sizes = dict(n_prompt=n_prompt, n_sample=n_sample)

    o_gate = 2 * A_QK + A_VW
    o_r = o_gate + A_RANK
    o_b = o_r + A_VW

    gla_p, gla_s, proj_in = [], [], []
    zero_state = jnp.zeros((batch, A_QK, A_DV), F32)
    for l in range(DEPTH):
        bf = lambda w: w.astype(BF)
        x = _ffn_call(x, row(norm_ffn1[l]), bf(w_ffn1_gate[l]), bf(w_ffn1_up[l]), bf(w_ffn1_down[l]), **sizes)

        wl = w_in[l]
        wgate = jnp.pad(wl[:, o_gate:o_r], ((0, 0), (0, LANES - A_RANK)))
        walpha = jnp.pad(w_alpha[l], ((0, LANES - A_RANK), (0, 0)))
        proj_in.append(x)
        aq, ak, av, la, ar, bq, bk, bv = _inproj_call(
            x, row(norm_mix[l]), bf(wl[:, :o_gate]), bf(wgate), bf(wl[:, o_r:o_b]), bf(wl[:, o_b:]),
            bf(walpha), row(b_alpha[l]))

        gn = row(gla_norm[l])
        g_p, sp = _gla_call(aq, ak, av, la, ar, zero_state, gn, row0=0, seq=seq, chunk=CHUNK,
                            group=GLA_GROUP, name="gla_prompt")
        g_s, ss = _gla_call(aq, ak, av, la, ar, state_gla[l].reshape(dec_batch, A_QK, A_DV), gn,
                            row0=n_prompt, seq=dec_seq, chunk=dec_seq, group=1, name="gla_sample")

        brow = jnp.pad(attn_bias[l][:, ::-1],
                       ((0, 0), (B_LEFT - CHUNK, BIAS_ROW - (B_LEFT - CHUNK) - (2 * REL_CLIP + 1))), mode="edge")
        brow = (brow - brow[:, :1]) * LOG2E
        gain = jnp.tile(row(attn_norm[l]), (1, 4))
        a_p = _attn_prompt_call(bq, bk, bv, brow, gain, n_streams=batch, seq=seq)
        a_s = _attn_sample_call(bq, bk, bv, cache_k[l].reshape(dec_batch, past, B_W),
                                cache_v[l].reshape(dec_batch, past, B_W), brow, gain, row0=n_prompt, tq=dec_seq)

        wo = bf(w_out[l])
        x = _ffn_call(x, row(norm_ffn2[l]), bf(w_ffn2_gate[l]), bf(w_ffn2_up[l]), bf(w_ffn2_down[l]),
                      mix=((g_p, g_s), (a_p, a_s), wo[:A_VW], wo[A_VW:]),
                      final_norm=row(final_norm) if l == DEPTH - 1 else None, **sizes)
        gla_p.append(sp.reshape(batch, A_HEADS, A_DK, A_DV))
        gla_s.append(ss.reshape(dec_batch, A_HEADS, A_DK, A_DV))

    k_p, v_p, k_s, v_s = _kv_tails_call(
        proj_in, norm_mix.reshape(DEPTH, 1, D_MODEL).astype(F32), w_in[:, :, o_b + B_W:].astype(BF),
        n_prompt=n_prompt, seq=seq)
    y_prompt, y_sample = x
    return (y_prompt.reshape(batch, seq, D_MODEL), y_sample.reshape(dec_batch, dec_seq, D_MODEL),
            jnp.stack(gla_p), k_p.reshape(DEPTH, batch, B_LEFT, B_HEADS, B_HD),
            v_p.reshape(DEPTH, batch, B_LEFT, B_HEADS, B_HD), jnp.stack(gla_s),
            k_s.reshape(DEPTH, dec_batch, dec_seq, B_HEADS, B_HD),
            v_s.reshape(DEPTH, dec_batch, dec_seq, B_HEADS, B_HD))
```

```python
import functools

import jax
import jax.numpy as jnp
from jax import lax
from jax.experimental import pallas as pl
from jax.experimental.pallas import tpu as pltpu

F32 = jnp.float32
BF = jnp.bfloat16

D_MODEL = 1024
D_FF = 2816
DEPTH = 2
CHUNK = 64
A_HEADS = 4
A_DK = 64
A_DV = 128
A_QK = A_HEADS * A_DK
A_VW = A_HEADS * A_DV
A_RANK = 16
A_TAU = 16.0
B_HEADS = 8
B_HD = 64
B_W = B_HEADS * B_HD
B_LEFT = 8 * CHUNK
REL_CLIP = 128
EPS = 1e-6

TM = 512
FF_CHUNK = 256
GLA_GROUP = 4
LANES = 128
BIAS_ROW = 768
BIAS_FLAT = B_LEFT - REL_CLIP
VMEM_LIMIT = 56 * 1024 * 1024
LOG2E = 1.4426950408889634
Q_SCALE = B_HD ** -0.5 * LOG2E


def _rmsnorm(x, g):
    ms = jnp.mean(x * x, axis=-1, keepdims=True)
    return x * lax.rsqrt(ms + EPS) * g


def _silu(x):
    return x * jax.nn.sigmoid(x)


def _dot(a, b):
    return jnp.dot(a, b, preferred_element_type=F32)


def _dot_nt(a, b):
    return lax.dot_general(a, b, (((1,), (1,)), ((), ())), preferred_element_type=F32)


def _const_spec(shape):
    return pl.BlockSpec(shape, lambda *_: (0,) * len(shape), pipeline_mode=pl.Buffered(1))


def _ffn_kernel(*refs, first, mix, final, prompt_tiles):
    refs = list(refs)
    i = pl.program_id(0)
    if first:
        xp_ref, xs_ref = refs[:2]
        refs = refs[2:]
        x = jnp.where(i < prompt_tiles, xp_ref[...], xs_ref[...])
    else:
        x = refs.pop(0)[...]
    if mix:
        gp_ref, gs_ref, ap_ref, as_ref, wog_ref, woa_ref = refs[:6]
        refs = refs[6:]
        g = jnp.where(i < prompt_tiles, gp_ref[...], gs_ref[...])
        a = jnp.where(i < prompt_tiles, ap_ref[...], as_ref[...])
        x = x + _dot(g, wog_ref[...]) + _dot(a, woa_ref[...])
    n_ref, wg_ref, wu_ref, wd_ref = refs[:4]
    refs = refs[4:]
    if final:
        fn_ref = refs.pop(0)
    act_ref = refs.pop()

    h = _rmsnorm(x, n_ref[...]).astype(BF)
    for c in range(D_FF // FF_CHUNK):
        sl = slice(c * FF_CHUNK, (c + 1) * FF_CHUNK)
        g = _dot(h, wg_ref[:, sl])
        u = _dot(h, wu_ref[:, sl])
        act_ref[:, sl] = (_silu(g) * u).astype(BF)
    y = x + 0.5 * _dot(act_ref[...], wd_ref[...])
    if not final:
        refs[0][...] = y
        return
    y = _rmsnorm(y, fn_ref[...])
    yp_ref, ys_ref = refs

    @pl.when(i < prompt_tiles)
    def _():
        yp_ref[...] = y

    @pl.when(i >= prompt_tiles)
    def _():
        ys_ref[...] = y


def _split_specs(width, prompt_tiles):
    return [pl.BlockSpec((TM, width), lambda i: (jnp.minimum(i, prompt_tiles - 1), 0)),
            pl.BlockSpec((TM, width), lambda i: (jnp.maximum(i - prompt_tiles, 0), 0))]


def _ffn_call(x, norm, wg, wu, wd, *, n_prompt, n_sample, mix=None, final_norm=None):
    n = n_prompt + n_sample
    prompt_tiles = n_prompt // TM
    first = isinstance(x, tuple)
    tok = lambda w: pl.BlockSpec((TM, w), lambda i: (i, 0))
    args = list(x) if first else [x]
    specs = _split_specs(D_MODEL, prompt_tiles) if first else [tok(D_MODEL)]
    if mix is not None:
        g, a, wog, woa = mix
        args += [*g, *a, wog, woa]
        specs += _split_specs(A_VW, prompt_tiles) + _split_specs(B_W, prompt_tiles)
        specs += [_const_spec((A_VW, D_MODEL)), _const_spec((B_W, D_MODEL))]
    args += [norm, wg, wu, wd]
    specs += [_const_spec((1, D_MODEL)), _const_spec((D_MODEL, D_FF)), _const_spec((D_MODEL, D_FF)),
              _const_spec((D_FF, D_MODEL))]
    if final_norm is not None:
        args.append(final_norm)
        specs.append(_const_spec((1, D_MODEL)))
        out_specs = _split_specs(D_MODEL, prompt_tiles)
        out_shape = [jax.ShapeDtypeStruct((n_prompt, D_MODEL), F32), jax.ShapeDtypeStruct((n_sample, D_MODEL), F32)]
    else:
        out_specs = tok(D_MODEL)
        out_shape = jax.ShapeDtypeStruct((n, D_MODEL), F32)
    return pl.pallas_call(
        functools.partial(_ffn_kernel, first=first, mix=mix is not None, final=final_norm is not None,
                          prompt_tiles=prompt_tiles),
        grid=(n // TM,),
        in_specs=specs,
        out_specs=out_specs,
        out_shape=out_shape,
        scratch_shapes=[pltpu.VMEM((TM, D_FF), BF)],
        compiler_params=pltpu.CompilerParams(dimension_semantics=("arbitrary",), vmem_limit_bytes=VMEM_LIMIT),
        name="ffn_mix" if mix is not None else "ffn",
    )(*args)


def _inproj_kernel(x_ref, n_ref, wa_ref, wgate_ref, wr_ref, wb_ref, walpha_ref, balpha_ref,
                   aq_ref, ak_ref, av_ref, la_ref, ar_ref, bq_ref, bk_ref, bv_ref):
    h = _rmsnorm(x_ref[...], n_ref[...]).astype(BF)
    ag = _dot(h, wgate_ref[...])
    za = _dot(h, wa_ref[...])
    pre = _dot(ag.astype(BF), walpha_ref[...]) + balpha_ref[...]
    aq_ref[...] = za[:, :A_QK]
    ak_ref[...] = za[:, A_QK:2 * A_QK]
    av_ref[...] = za[:, 2 * A_QK:].astype(BF)
    ar_ref[...] = _dot(h, wr_ref[...])
    log_sig = jnp.minimum(pre, 0.0) - jnp.log1p(jnp.exp(-jnp.abs(pre)))
    la_ref[...] = log_sig * (1.0 / A_TAU)
    zb = _dot(h, wb_ref[...])
    bq_ref[...] = (zb[:, :B_W] * Q_SCALE).astype(BF)
    bk_ref[...] = zb[:, B_W:2 * B_W].astype(BF)
    bv_ref[...] = zb[:, 2 * B_W:].astype(BF)


def _inproj_call(x, norm, wa, wgate, wr, wb, walpha, balpha):
    n = x.shape[0]
    tok = lambda w: pl.BlockSpec((TM, w), lambda i: (i, 0))
    outs = [
        (A_QK, F32), (A_QK, F32), (A_VW, BF), (A_QK, F32), (A_VW, F32), (B_W, BF), (B_W, BF), (B_W, BF)]
    return pl.pallas_call(
        _inproj_kernel,
        grid=(n // TM,),
        in_specs=[tok(D_MODEL), _const_spec((1, D_MODEL)), _const_spec(wa.shape), _const_spec(wgate.shape),
                  _const_spec(wr.shape), _const_spec(wb.shape), _const_spec(walpha.shape),
                  _const_spec(balpha.shape)],
        out_specs=[tok(w) for w, _ in outs],
        out_shape=[jax.ShapeDtypeStruct((n, w), dt) for w, dt in outs],
        compiler_params=pltpu.CompilerParams(dimension_semantics=("arbitrary",), vmem_limit_bytes=VMEM_LIMIT),
        name="inproj",
    )(x, norm, wa, wgate, wr, wb, walpha, balpha)


def _kv_tails_kernel(*refs, prompt_tail_tiles):
    x_refs = refs[:DEPTH]
    n_ref, w_ref, kp_ref, vp_ref, ks_ref, vs_ref = refs[DEPTH:]
    layer = pl.program_id(0)
    t = pl.program_id(1)
    x = x_refs[0][...]
    for l in range(1, DEPTH):
        x = jnp.where(layer == l, x_refs[l][...], x)
    kv = _dot(_rmsnorm(x, n_ref[...]).astype(BF), w_ref[...])

    def put(k_out, v_out):
        for hd in range(B_HEADS):
            k_out[pl.ds(hd, TM, stride=B_HEADS), :] = kv[:, hd * B_HD:(hd + 1) * B_HD]
            v_out[pl.ds(hd, TM, stride=B_HEADS), :] = kv[:, B_W + hd * B_HD:B_W + (hd + 1) * B_HD]

    @pl.when(t < prompt_tail_tiles)
    def _():
        put(kp_ref, vp_ref)

    @pl.when(t >= prompt_tail_tiles)
    def _():
        put(ks_ref, vs_ref)


def _kv_tails_call(xs, norms, w_kv, *, n_prompt, seq):
    n = xs[0].shape[0]
    tiles_per_stream = seq // TM
    prompt_tiles = n_prompt // TM
    n_streams = n_prompt // seq
    sample_tiles = n // TM - prompt_tiles
    tail_tile = lambda t: jnp.where(t < n_streams, t * tiles_per_stream + tiles_per_stream - 1,
                                    t - n_streams + prompt_tiles)
    tail_tok = [pl.BlockSpec((TM, D_MODEL), lambda l, t, own=own: (jnp.where(l == own, tail_tile(t), 0), 0))
                for own in range(DEPTH)]
    rows = TM * B_HEADS
    p_out = pl.BlockSpec((None, rows, B_HD), lambda l, t: (l, jnp.minimum(t, n_streams - 1), 0))
    s_out = pl.BlockSpec((None, rows, B_HD), lambda l, t: (l, jnp.maximum(t - n_streams, 0), 0))
    p_shape = jax.ShapeDtypeStruct((DEPTH, n_streams * rows, B_HD), F32)
    s_shape = jax.ShapeDtypeStruct((DEPTH, sample_tiles * rows, B_HD), F32)
    return pl.pallas_call(
        functools.partial(_kv_tails_kernel, prompt_tail_tiles=n_streams),
        grid=(DEPTH, n_streams + sample_tiles),
        in_specs=tail_tok + [pl.BlockSpec((None, 1, D_MODEL), lambda l, t: (l, 0, 0)),
                                       pl.BlockSpec((None, D_MODEL, 2 * B_W), lambda l, t: (l, 0, 0))],
        out_specs=[p_out, p_out, s_out, s_out],
        out_shape=[p_shape, p_shape, s_shape, s_shape],
        compiler_params=pltpu.CompilerParams(dimension_semantics=("arbitrary", "arbitrary"),
                                             vmem_limit_bytes=VMEM_LIMIT),
        name="kv_tails",
    )(*xs, norms, w_kv)


def _gla_kernel(q_ref, k_ref, v_ref, la_ref, r_ref, s0_ref, gn_ref, g_ref, sout_ref,
                qbd_ref, kdec_ref, inc_ref, dec_ref, s_ref, *, chunk, group, n_sub):
    c = chunk
    rows = group * c
    r_i = lax.broadcasted_iota(jnp.int32, (rows, rows), 0)
    c_i = lax.broadcasted_iota(jnp.int32, (rows, rows), 1)
    same_chunk = (r_i & -c) == (c_i & -c)
    tri = jnp.where(same_chunk & (r_i >= c_i), 1.0, 0.0).astype(BF)
    causal = lax.broadcasted_iota(jnp.int32, (c, c), 0) >= lax.broadcasted_iota(jnp.int32, (c, c), 1)
    lane = lax.broadcasted_iota(jnp.int32, (c, A_QK), 1)
    head_mask = [(lane >= h * A_DK) & (lane < (h + 1) * A_DK) for h in range(A_HEADS)]
    pair = rows >= LANES
    if pair:
        pos = lax.broadcasted_iota(jnp.int32, (A_DK, LANES), 1)
        pos_mask = [pos < c, pos >= c]
    tail = 8 if group * 8 >= 32 else c
    gn = gn_ref[...]

    def sub_rows(t):
        start = t * rows
        return start if isinstance(start, int) else pl.multiple_of(start, rows)

    def step(te, se, tp, sp):
        if tp is not None:
            rp = pl.ds(sub_rows(tp), rows)
            la = la_ref[rp, :]
            la_hi = la.astype(BF)
            rem = la - la_hi.astype(F32)
            la_mid = rem.astype(BF)
            la_lo = (rem - la_mid.astype(F32)).astype(BF)
            b3 = _dot(tri, jnp.concatenate([la_hi, la_mid, la_lo], axis=1))
        if te is not None:
            row0 = sub_rows(te)
            re = pl.ds(row0, rows)
            states = [s_ref[...]]
            for ci in range(group):
                states.append(dec_ref[se, ci] * states[-1] + inc_ref[se, ci])
            s_ref[...] = states[-1]
            q_bd = [qbd_ref[se, ci] for ci in range(group)]
            scores = [_dot_nt(q_bd[ci], kdec_ref[se, ci * c:(ci + 1) * c, :]) for ci in range(group)]
            o_inter = [_dot(q_bd[ci], states[ci].astype(BF)) for ci in range(group)]
        if tp is not None:
            b = b3[:, :A_QK] + b3[:, A_QK:2 * A_QK] + b3[:, 2 * A_QK:]
            k = k_ref[rp, :]
            q_dec = (q_ref[rp, :] * (A_DK ** -0.5) * jnp.exp(b)).astype(BF)
            kdec_ref[sp] = (k * jnp.exp(-b)).astype(BF)
            b_last = jnp.concatenate(
                [jnp.broadcast_to(b[(ci + 1) * c - 1:(ci + 1) * c, :], (c, A_QK)) for ci in range(group)], axis=0)
            ku_t = (k * jnp.exp(b_last - b)).T.astype(BF)
            tail_t = jnp.concatenate([b[(ci + 1) * c - tail:(ci + 1) * c, :] for ci in range(group)], axis=0).T
            for ci in range(group):
                q_c = q_dec[ci * c:(ci + 1) * c, :]
                qbd_ref[sp, ci] = jnp.concatenate(
                    [jnp.where(head_mask[h], q_c, jnp.zeros_like(q_c)) for h in range(A_HEADS)], axis=0)
                decay = jnp.exp(tail_t[:, (ci + 1) * tail - 1:(ci + 1) * tail])
                dec_ref[sp, ci] = jnp.broadcast_to(decay, (A_QK, A_DV))
        if te is not None:
            v = v_ref[re, :]
            r = r_ref[re, :]
            for ci in range(group):
                cr = slice(ci * c, (ci + 1) * c)
                outs = []
                for h in range(A_HEADS):
                    hs = slice(h * c, (h + 1) * c)
                    hv = slice(h * A_DV, (h + 1) * A_DV)
                    s_h = jnp.where(causal, scores[ci][hs, :], 0.0).astype(BF)
                    o = _dot(s_h, v[cr, hv]) + o_inter[ci][hs, :]
                    outs.append((_rmsnorm(o, gn) * _silu(r[cr, hv])).astype(BF))
                g_ref[pl.ds(row0 + ci * c, c), :] = jnp.concatenate(outs, axis=1)
        if tp is not None:
            v = v_ref[rp, :]
            for ci in range(group):
                incs = []
                for h in range(A_HEADS):
                    hk = slice(h * A_DK, (h + 1) * A_DK)
                    hv = slice(h * A_DV, (h + 1) * A_DV)
                    if pair:
                        p0 = (ci // 2) * LANES
                        lhs = ku_t[hk, p0:p0 + LANES]
                        lhs = jnp.where(pos_mask[ci % 2], lhs, jnp.zeros_like(lhs))
                        incs.append(_dot(lhs, v[p0:p0 + LANES, hv]))
                    else:
                        incs.append(_dot(ku_t[hk, :], v[:, hv]))
                inc_ref[sp, ci] = jnp.concatenate(incs, axis=0)

    s_ref[...] = s0_ref[...]
    step(None, None, 0, 0)
    if n_sub > 1:
        def body(j, carry):
            t = 2 * j
            step(t, 0, t + 1, 1)
            step(t + 1, 1, t + 2, 0)
            return carry

        lax.fori_loop(0, n_sub // 2 - 1, body, 0)
        step(n_sub - 2, 0, n_sub - 1, 1)
        step(n_sub - 1, 1, None, None)
    else:
        step(0, 0, None, None)
    sout_ref[...] = s_ref[...]


def _gla_call(aq, ak, av, la, ar, s0, gn, *, row0, seq, chunk, group, name):
    n_streams = s0.shape[0]
    rows = chunk * group
    n_sub = seq // rows
    assert seq % rows == 0 and (n_sub == 1 or n_sub % 2 == 0) and chunk & (chunk - 1) == 0
    base = row0 // seq
    tok = lambda w: pl.BlockSpec((seq, w), lambda b: (base + b, 0))
    st = pl.BlockSpec((None, A_QK, A_DV), lambda b: (b, 0, 0))
    return pl.pallas_call(
        functools.partial(_gla_kernel, chunk=chunk, group=group, n_sub=n_sub),
        grid=(n_streams,),
        in_specs=[tok(A_QK), tok(A_QK), tok(A_VW), tok(A_QK), tok(A_VW), st,
                  pl.BlockSpec((1, A_DV), lambda b: (0, 0))],
        out_specs=[pl.BlockSpec((seq, A_VW), lambda b: (b, 0)), st],
        out_shape=[jax.ShapeDtypeStruct((n_streams * seq, A_VW), BF),
                   jax.ShapeDtypeStruct((n_streams, A_QK, A_DV), F32)],
        scratch_shapes=[pltpu.VMEM((2, group, A_HEADS * chunk, A_QK), BF), pltpu.VMEM((2, rows, A_QK), BF),
                        pltpu.VMEM((2, group, A_QK, A_DV), F32), pltpu.VMEM((2, group, A_QK, A_DV), F32),
                        pltpu.VMEM((A_QK, A_DV), F32)],
        compiler_params=pltpu.CompilerParams(dimension_semantics=("arbitrary",), vmem_limit_bytes=VMEM_LIMIT),
        name=name,
    )(aq, ak, av, la, ar, s0, gn)


def _bias_block(brow_ref, blk, tq, tk):
    rows = []
    for h in range(4):
        f = jnp.broadcast_to(brow_ref[blk * 4 + h:blk * 4 + h + 1, :], (tq, BIAS_ROW))
        rows.append(pltpu.roll(f, BIAS_ROW - CHUNK, 1, stride=1, stride_axis=0)[:, :tk])
    return jnp.concatenate(rows, axis=0)


def _block_diag_rows(qb):
    lane = lax.broadcasted_iota(jnp.int32, qb.shape, 1)
    return jnp.concatenate(
        [jnp.where((lane >= h * B_HD) & (lane < (h + 1) * B_HD), qb, jnp.zeros_like(qb)) for h in range(4)],
        axis=0)


def _attn_scores(qb, kw, valid):
    s = _dot_nt(_block_diag_rows(qb), kw)
    return s if valid is None else jnp.where(valid, s, -1e30)


def _attn_softmax(s, bias_tail):
    s = jnp.concatenate([s[:, :BIAS_FLAT], s[:, BIAS_FLAT:] + bias_tail], axis=1)
    p = jnp.exp2(s - jnp.max(s, axis=-1, keepdims=True))
    l = jnp.sum(p, axis=-1, keepdims=True)
    return p.astype(BF), jnp.broadcast_to(l, (s.shape[0], LANES))


def _attn_norm(o, l, gain, diag, tq):
    o = jnp.where(diag, o, 0.0)
    inv = 1.0 / l
    ms = jnp.sum(o * o, axis=-1, keepdims=True) * (inv * inv * (1.0 / B_HD))
    f = inv * lax.rsqrt(ms + EPS)
    y = o * jnp.concatenate([f, f], axis=1)
    return (y[0:tq] + y[tq:2 * tq] + y[2 * tq:3 * tq] + y[3 * tq:4 * tq]) * gain


def _diag_mask(tq):
    r = lax.broadcasted_iota(jnp.int32, (4 * tq, 4 * B_HD), 0)
    cidx = lax.broadcasted_iota(jnp.int32, (4 * tq, 4 * B_HD), 1)
    m = None
    for h in range(4):
        t = (r >= h * tq) & (r < (h + 1) * tq) & (cidx >= h * B_HD) & (cidx < (h + 1) * B_HD)
        m = t if m is None else (m | t)
    return m


def _attn_prompt_kernel(q_ref, k_ref, v_ref, brow_ref, gain_ref, o_ref,
                        kpad_ref, vpad_ref, bias_ref, s_ref, p_ref, l_ref, acc_ref, *, seq):
    tk = B_LEFT + CHUNK
    n_chunks = seq // CHUNK
    kpad_ref[0:B_LEFT, :] = jnp.zeros((B_LEFT, B_W), BF)
    vpad_ref[0:B_LEFT, :] = jnp.zeros((B_LEFT, B_W), BF)
    kpad_ref[B_LEFT:, :] = k_ref[...]
    vpad_ref[B_LEFT:, :] = v_ref[...]
    for blk in range(2):
        bias_ref[blk] = _bias_block(brow_ref, blk, CHUNK, tk)[:, BIAS_FLAT:]
    gain = gain_ref[...]
    diag = _diag_mask(CHUNK)
    key_pos = lax.broadcasted_iota(jnp.int32, (1, tk), 1)
    blocks = [slice(blk * 4 * B_HD, (blk + 1) * 4 * B_HD) for blk in range(2)]

    def scores_stage(ci, slot, masked, blk):
        r0 = pl.multiple_of(ci * CHUNK, CHUNK)
        valid = (key_pos + r0 >= B_LEFT) if masked else None
        cs = blocks[blk]
        s_ref[slot, blk] = _attn_scores(q_ref[pl.ds(r0, CHUNK), cs], kpad_ref[pl.ds(r0, tk), cs], valid)

    def softmax_stage(slot, blk):
        p, l = _attn_softmax(s_ref[slot, blk], bias_ref[blk])
        p_ref[slot, blk] = p
        l_ref[slot, blk] = l

    def values_stage(ci, slot, blk):
        r0 = pl.multiple_of(ci * CHUNK, CHUNK)
        acc_ref[slot, blk] = _dot(p_ref[slot, blk], vpad_ref[pl.ds(r0, tk), blocks[blk]])

    def norm_stage(ci, slot, blk):
        r0 = pl.multiple_of(ci * CHUNK, CHUNK)
        o_ref[pl.ds(r0, CHUNK), blocks[blk]] = _attn_norm(
            acc_ref[slot, blk], l_ref[slot, blk], gain, diag, CHUNK).astype(BF)

    def step(i, parity, masked, live):
        for blk in range(2):
            if live[0]:
                scores_stage(i, parity, masked, blk)
            if live[3]:
                norm_stage(i - 3, 1 - parity, blk)
            if live[1]:
                softmax_stage(1 - parity, blk)
            if live[2]:
                values_stage(i - 2, parity, blk)

    def static_step(i, masked):
        step(i, i % 2, masked, [0 <= i - k < n_chunks for k in range(4)])

    def pair(masked):
        def body(j, carry):
            for d in range(2):
                step(2 * j + d, d, masked, [True] * 4)
            return carry
        return body

    n_masked = B_LEFT // CHUNK
    for i in range(4):
        static_step(i, True)
    lax.fori_loop(2, n_masked // 2, pair(True), 0)
    lax.fori_loop(n_masked // 2, n_chunks // 2, pair(False), 0)
    for i in range(n_chunks, n_chunks + 3):
        static_step(i, False)


def _attn_prompt_call(bq, bk, bv, brow, gain, *, n_streams, seq):
    tok = pl.BlockSpec((seq, B_W), lambda b: (b, 0))
    return pl.pallas_call(
        functools.partial(_attn_prompt_kernel, seq=seq),
        grid=(n_streams,),
        in_specs=[tok, tok, tok, pl.BlockSpec(brow.shape, lambda b: (0, 0)),
                  pl.BlockSpec(gain.shape, lambda b: (0, 0))],
        out_specs=tok,
        out_shape=jax.ShapeDtypeStruct((n_streams * seq, B_W), BF),
        scratch_shapes=[pltpu.VMEM((B_LEFT + seq, B_W), BF), pltpu.VMEM((B_LEFT + seq, B_W), BF),
                        pltpu.VMEM((2, 4 * CHUNK, B_LEFT + CHUNK - BIAS_FLAT), F32),
                        pltpu.VMEM((2, 2, 4 * CHUNK, B_LEFT + CHUNK), F32),
                        pltpu.VMEM((2, 2, 4 * CHUNK, B_LEFT + CHUNK), BF),
                        pltpu.VMEM((2, 2, 4 * CHUNK, LANES), F32),
                        pltpu.VMEM((2, 2, 4 * CHUNK, 4 * B_HD), F32)],
        compiler_params=pltpu.CompilerParams(dimension_semantics=("arbitrary",), vmem_limit_bytes=VMEM_LIMIT),
        name="attn_prompt",
    )(bq, bk, bv, brow, gain)


def _attn_sample_kernel(q_ref, k_ref, v_ref, ck_ref, cv_ref, brow_ref, gain_ref, o_ref, *, tq):
    w = ck_ref.shape[0] // B_HEADS
    q = q_ref[...].astype(F32)
    k_new = k_ref[...].astype(F32)
    v_new = v_ref[...].astype(F32)
    gain = gain_ref[:, :B_HD]
    outs = []
    for h in range(B_HEADS):
        cols = slice(h * B_HD, (h + 1) * B_HD)
        head_rows = pl.ds(h, w, stride=B_HEADS)
        keys = jnp.concatenate([ck_ref[head_rows, :], k_new[:, cols]], axis=0).astype(BF)
        vals = jnp.concatenate([cv_ref[head_rows, :], v_new[:, cols]], axis=0).astype(BF)
        f = jnp.broadcast_to(brow_ref[h:h + 1, :], (tq, BIAS_ROW))
        bias = pltpu.roll(f, BIAS_ROW - CHUNK, 1, stride=1, stride_axis=0)[:, :w + tq]
        s = _dot_nt(q[:, cols].astype(BF), keys) + bias
        p = jnp.exp2(s - jnp.max(s, axis=-1, keepdims=True))
        o = _dot(p.astype(BF), vals) / jnp.sum(p, axis=-1, keepdims=True)
        outs.append(_rmsnorm(o, gain))
    o_ref[...] = jnp.concatenate(outs, axis=1).astype(BF)


def _attn_sample_call(bq, bk, bv, cache_k, cache_v, brow, gain, *, layer, row0, tq):
    _, n_streams, rows, _ = cache_k.shape
    base = row0 // tq
    tok = pl.BlockSpec((tq, B_W), lambda b: (base + b, 0))
    cache = pl.BlockSpec((None, None, rows, B_HD), lambda b: (layer, b, 0, 0))
    return pl.pallas_call(
        functools.partial(_attn_sample_kernel, tq=tq),
        grid=(n_streams,),
        in_specs=[tok, tok, tok, cache, cache, pl.BlockSpec(brow.shape, lambda b: (0, 0)),
                  pl.BlockSpec(gain.shape, lambda b: (0, 0))],
        out_specs=pl.BlockSpec((tq, B_W), lambda b: (b, 0)),
        out_shape=jax.ShapeDtypeStruct((n_streams * tq, B_W), BF),
        compiler_params=pltpu.CompilerParams(dimension_semantics=("arbitrary",), vmem_limit_bytes=VMEM_LIMIT),
        name="attn_sample",
    )(bq, bk, bv, cache_k, cache_v, brow, gain)


def kernel(x_prompt, x_sample, state_gla, cache_k, cache_v, norm_ffn1, w_ffn1_gate, w_ffn1_up, w_ffn1_down,
           norm_mix, w_in, w_alpha, b_alpha, gla_norm, attn_bias, attn_norm, w_out, norm_ffn2, w_ffn2_gate,
           w_ffn2_up, w_ffn2_down, final_norm):
    batch, seq, _ = x_prompt.shape
    dec_batch, dec_seq, _ = x_sample.shape
    n_prompt = batch * seq
    n_sample = dec_batch * dec_seq
    past = cache_k.shape[2]
    assert seq % TM == 0 and n_sample % TM == 0 and TM == B_LEFT and past == B_LEFT
    assert dec_seq <= CHUNK
    assert seq % (2 * CHUNK) == 0 and seq >= B_LEFT + 2 * CHUNK

    x = (x_prompt.reshape(n_prompt, D_MODEL), x_sample.reshape(n_sample, D_MODEL))
    row = lambda v: v.reshape(1, -1).astype(F32)
    sizes = dict(n_prompt=n_prompt, n_sample=n_sample)
    cache_rows = lambda c: c.reshape(DEPTH, dec_batch, past * B_HEADS, B_HD)

    o_gate = 2 * A_QK + A_VW
    o_r = o_gate + A_RANK
    o_b = o_r + A_VW

    gla_p, gla_s, proj_in = [], [], []
    zero_state = jnp.zeros((batch, A_QK, A_DV), F32)
    for l in range(DEPTH):
        bf = lambda w: w.astype(BF)
        x = _ffn_call(x, row(norm_ffn1[l]), bf(w_ffn1_gate[l]), bf(w_ffn1_up[l]), bf(w_ffn1_down[l]), **sizes)

        wl = w_in[l]
        wgate = jnp.pad(wl[:, o_gate:o_r], ((0, 0), (0, LANES - A_RANK)))
        walpha = jnp.pad(w_alpha[l], ((0, LANES - A_RANK), (0, 0)))
        proj_in.append(x)
        aq, ak, av, la, ar, bq, bk, bv = _inproj_call(
            x, row(norm_mix[l]), bf(wl[:, :o_gate]), bf(wgate), bf(wl[:, o_r:o_b]), bf(wl[:, o_b:]),
            bf(walpha), row(b_alpha[l]))

        gn = row(gla_norm[l])
        g_p, sp = _gla_call(aq, ak, av, la, ar, zero_state, gn, row0=0, seq=seq, chunk=CHUNK,
                            group=GLA_GROUP, name="gla_prompt")
        g_s, ss = _gla_call(aq, ak, av, la, ar, state_gla[l].reshape(dec_batch, A_QK, A_DV), gn,
                            row0=n_prompt, seq=dec_seq, chunk=dec_seq, group=1, name="gla_sample")

        brow = jnp.pad(attn_bias[l][:, ::-1],
                       ((0, 0), (B_LEFT - CHUNK, BIAS_ROW - (B_LEFT - CHUNK) - (2 * REL_CLIP + 1))), mode="edge")
        brow = (brow - brow[:, :1]) * LOG2E
        gain = jnp.tile(row(attn_norm[l]), (1, 4))
        a_p = _attn_prompt_call(bq, bk, bv, brow, gain, n_streams=batch, seq=seq)
        a_s = _attn_sample_call(bq, bk, bv, cache_rows(cache_k), cache_rows(cache_v), brow, gain,
                                layer=l, row0=n_prompt, tq=dec_seq)

        wo = bf(w_out[l])
        x = _ffn_call(x, row(norm_ffn2[l]), bf(w_ffn2_gate[l]), bf(w_ffn2_up[l]), bf(w_ffn2_down[l]),
                      mix=((g_p, g_s), (a_p, a_s), wo[:A_VW], wo[A_VW:]),
                      final_norm=row(final_norm) if l == DEPTH - 1 else None, **sizes)
        gla_p.append(sp.reshape(batch, A_HEADS, A_DK, A_DV))
        gla_s.append(ss.reshape(dec_batch, A_HEADS, A_DK, A_DV))

    k_p, v_p, k_s, v_s = _kv_tails_call(
        proj_in, norm_mix.reshape(DEPTH, 1, D_MODEL).astype(F32), w_in[:, :, o_b + B_W:].astype(BF),
        n_prompt=n_prompt, seq=seq)
    y_prompt, y_sample = x
    return (y_prompt.reshape(batch, seq, D_MODEL), y_sample.reshape(dec_batch, dec_seq, D_MODEL),
            jnp.stack(gla_p), k_p.reshape(DEPTH, batch, B_LEFT, B_HEADS, B_HD),
            v_p.reshape(DEPTH, batch, B_LEFT, B_HEADS, B_HD), jnp.stack(gla_s),
            k_s.reshape(DEPTH, dec_batch, dec_seq, B_HEADS, B_HD),
            v_s.reshape(DEPTH, dec_batch, dec_seq, B_HEADS, B_HD))
```

```python
import functools

import jax
import jax.numpy as jnp
from jax import lax
from jax.experimental import pallas as pl
from jax.experimental.pallas import tpu as pltpu

F32 = jnp.float32
BF = jnp.bfloat16

D_MODEL = 1024
D_FF = 2816
DEPTH = 2
CHUNK = 64
A_HEADS = 4
A_DK = 64
A_DV = 128
A_QK = A_HEADS * A_DK
A_VW = A_HEADS * A_DV
A_RANK = 16
A_TAU = 16.0
B_HEADS = 8
B_HD = 64
B_W = B_HEADS * B_HD
B_LEFT = 8 * CHUNK
REL_CLIP = 128
EPS = 1e-6

TM = 512
FF_CHUNK = 256
GLA_GROUP = 4
LANES = 128
BIAS_ROW = 768
BIAS_FLAT = B_LEFT - REL_CLIP
VMEM_LIMIT = 56 * 1024 * 1024
LOG2E = 1.4426950408889634
Q_SCALE = B_HD ** -0.5 * LOG2E


def _rmsnorm(x, g):
    ms = jnp.mean(x * x, axis=-1, keepdims=True)
    return x * lax.rsqrt(ms + EPS) * g


def _silu(x):
    return x * jax.nn.sigmoid(x)


def _dot(a, b):
    return jnp.dot(a, b, preferred_element_type=F32)


def _dot_nt(a, b):
    return lax.dot_general(a, b, (((1,), (1,)), ((), ())), preferred_element_type=F32)


def _const_spec(shape):
    return pl.BlockSpec(shape, lambda *_: (0,) * len(shape), pipeline_mode=pl.Buffered(1))


def _ffn_kernel(*refs, first, mix, final, prompt_tiles):
    refs = list(refs)
    i = pl.program_id(0)
    if first:
        xp_ref, xs_ref = refs[:2]
        refs = refs[2:]
        x = jnp.where(i < prompt_tiles, xp_ref[...], xs_ref[...])
    else:
        x = refs.pop(0)[...]
    if mix:
        gp_ref, gs_ref, ap_ref, as_ref, wog_ref, woa_ref = refs[:6]
        refs = refs[6:]
        g = jnp.where(i < prompt_tiles, gp_ref[...], gs_ref[...])
        a = jnp.where(i < prompt_tiles, ap_ref[...], as_ref[...])
        x = x + _dot(g, wog_ref[...]) + _dot(a, woa_ref[...])
    n_ref, wg_ref, wu_ref, wd_ref = refs[:4]
    refs = refs[4:]
    if final:
        fn_ref = refs.pop(0)
    act_ref = refs.pop()

    h = _rmsnorm(x, n_ref[...]).astype(BF)
    for c in range(D_FF // FF_CHUNK):
        sl = slice(c * FF_CHUNK, (c + 1) * FF_CHUNK)
        g = _dot(h, wg_ref[:, sl])
        u = _dot(h, wu_ref[:, sl])
        act_ref[:, sl] = (_silu(g) * u).astype(BF)
    y = x + 0.5 * _dot(act_ref[...], wd_ref[...])
    if not final:
        refs[0][...] = y
        return
    y = _rmsnorm(y, fn_ref[...])
    yp_ref, ys_ref = refs

    @pl.when(i < prompt_tiles)
    def _():
        yp_ref[...] = y

    @pl.when(i >= prompt_tiles)
    def _():
        ys_ref[...] = y


def _split_specs(width, prompt_tiles):
    return [pl.BlockSpec((TM, width), lambda i: (jnp.minimum(i, prompt_tiles - 1), 0)),
            pl.BlockSpec((TM, width), lambda i: (jnp.maximum(i - prompt_tiles, 0), 0))]


def _ffn_call(x, norm, wg, wu, wd, *, n_prompt, n_sample, mix=None, final_norm=None):
    n = n_prompt + n_sample
    prompt_tiles = n_prompt // TM
    first = isinstance(x, tuple)
    tok = lambda w: pl.BlockSpec((TM, w), lambda i: (i, 0))
    args = list(x) if first else [x]
    specs = _split_specs(D_MODEL, prompt_tiles) if first else [tok(D_MODEL)]
    if mix is not None:
        g, a, wog, woa = mix
        args += [*g, *a, wog, woa]
        specs += _split_specs(A_VW, prompt_tiles) + _split_specs(B_W, prompt_tiles)
        specs += [_const_spec((A_VW, D_MODEL)), _const_spec((B_W, D_MODEL))]
    args += [norm, wg, wu, wd]
    specs += [_const_spec((1, D_MODEL)), _const_spec((D_MODEL, D_FF)), _const_spec((D_MODEL, D_FF)),
              _const_spec((D_FF, D_MODEL))]
    if final_norm is not None:
        args.append(final_norm)
        specs.append(_const_spec((1, D_MODEL)))
        out_specs = _split_specs(D_MODEL, prompt_tiles)
        out_shape = [jax.ShapeDtypeStruct((n_prompt, D_MODEL), F32), jax.ShapeDtypeStruct((n_sample, D_MODEL), F32)]
    else:
        out_specs = tok(D_MODEL)
        out_shape = jax.ShapeDtypeStruct((n, D_MODEL), F32)
    return pl.pallas_call(
        functools.partial(_ffn_kernel, first=first, mix=mix is not None, final=final_norm is not None,
                          prompt_tiles=prompt_tiles),
        grid=(n // TM,),
        in_specs=specs,
        out_specs=out_specs,
        out_shape=out_shape,
        scratch_shapes=[pltpu.VMEM((TM, D_FF), BF)],
        compiler_params=pltpu.CompilerParams(dimension_semantics=("arbitrary",), vmem_limit_bytes=VMEM_LIMIT),
        name="ffn_mix" if mix is not None else "ffn",
    )(*args)


def _inproj_kernel(x_ref, n_ref, wa_ref, wgate_ref, wr_ref, wb_ref, walpha_ref, balpha_ref,
                   aq_ref, ak_ref, av_ref, la_ref, ar_ref, bq_ref, bk_ref, bv_ref):
    h = _rmsnorm(x_ref[...], n_ref[...]).astype(BF)
    ag = _dot(h, wgate_ref[...])
    za = _dot(h, wa_ref[...])
    pre = _dot(ag.astype(BF), walpha_ref[...]) + balpha_ref[...]
    aq_ref[...] = za[:, :A_QK]
    ak_ref[...] = za[:, A_QK:2 * A_QK]
    av_ref[...] = za[:, 2 * A_QK:].astype(BF)
    ar_ref[...] = _dot(h, wr_ref[...])
    log_sig = jnp.minimum(pre, 0.0) - jnp.log1p(jnp.exp(-jnp.abs(pre)))
    la_ref[...] = log_sig * (1.0 / A_TAU)
    zb = _dot(h, wb_ref[...])
    bq_ref[...] = (zb[:, :B_W] * Q_SCALE).astype(BF)
    bk_ref[...] = zb[:, B_W:2 * B_W].astype(BF)
    bv_ref[...] = zb[:, 2 * B_W:].astype(BF)


def _inproj_call(x, norm, wa, wgate, wr, wb, walpha, balpha):
    n = x.shape[0]
    tok = lambda w: pl.BlockSpec((TM, w), lambda i: (i, 0))
    outs = [
        (A_QK, F32), (A_QK, F32), (A_VW, BF), (A_QK, F32), (A_VW, F32), (B_W, BF), (B_W, BF), (B_W, BF)]
    return pl.pallas_call(
        _inproj_kernel,
        grid=(n // TM,),
        in_specs=[tok(D_MODEL), _const_spec((1, D_MODEL)), _const_spec(wa.shape), _const_spec(wgate.shape),
                  _const_spec(wr.shape), _const_spec(wb.shape), _const_spec(walpha.shape),
                  _const_spec(balpha.shape)],
        out_specs=[tok(w) for w, _ in outs],
        out_shape=[jax.ShapeDtypeStruct((n, w), dt) for w, dt in outs],
        compiler_params=pltpu.CompilerParams(dimension_semantics=("arbitrary",), vmem_limit_bytes=VMEM_LIMIT),
        name="inproj",
    )(x, norm, wa, wgate, wr, wb, walpha, balpha)


def _kv_tails_kernel(*refs, transposed):
    x_refs = refs[:DEPTH]
    n_ref, w_ref, k_ref, v_ref = refs[DEPTH:]
    layer = pl.program_id(0)
    x = x_refs[0][...]
    for l in range(1, DEPTH):
        x = jnp.where(layer == l, x_refs[l][...], x)
    h = _rmsnorm(x, n_ref[...]).astype(BF)
    if transposed:
        kv = _dot_nt(w_ref[...], h)
        k_ref[...] = kv[:B_W].reshape(B_HEADS, B_HD, TM)
        v_ref[...] = kv[B_W:].reshape(B_HEADS, B_HD, TM)
    else:
        kv = _dot(h, w_ref[...])
        k_ref[...] = kv[:, :B_W]
        v_ref[...] = kv[:, B_W:]


def _kv_tails_call(xs, norms, w_kv, *, first_tile, tile_step, n_tiles, transposed):
    tok = [pl.BlockSpec((TM, D_MODEL),
                        lambda l, t, own=own: (jnp.where(l == own, first_tile + t * tile_step, first_tile), 0))
           for own in range(DEPTH)]
    if transposed:
        out = pl.BlockSpec((None, None, B_HEADS, B_HD, TM), lambda l, t: (l, t, 0, 0, 0))
        shape = jax.ShapeDtypeStruct((DEPTH, n_tiles, B_HEADS, B_HD, TM), F32)
    else:
        out = pl.BlockSpec((None, TM, B_W), lambda l, t: (l, t, 0))
        shape = jax.ShapeDtypeStruct((DEPTH, n_tiles * TM, B_W), F32)
    return pl.pallas_call(
        functools.partial(_kv_tails_kernel, transposed=transposed),
        grid=(DEPTH, n_tiles),
        in_specs=tok + [pl.BlockSpec((None, 1, D_MODEL), lambda l, t: (l, 0, 0)),
                        pl.BlockSpec((None,) + w_kv.shape[1:], lambda l, t: (l, 0, 0))],
        out_specs=[out, out],
        out_shape=[shape, shape],
        compiler_params=pltpu.CompilerParams(dimension_semantics=("arbitrary", "arbitrary"),
                                             vmem_limit_bytes=VMEM_LIMIT),
        name="kv_tails_prompt" if transposed else "kv_tails_sample",
    )(*xs, norms, w_kv)


def _gla_kernel(q_ref, k_ref, v_ref, la_ref, r_ref, s0_ref, gn_ref, g_ref, sout_ref,
                qbd_ref, kdec_ref, inc_ref, dec_ref, s_ref, *, chunk, group, n_sub):
    c = chunk
    rows = group * c
    r_i = lax.broadcasted_iota(jnp.int32, (rows, rows), 0)
    c_i = lax.broadcasted_iota(jnp.int32, (rows, rows), 1)
    same_chunk = (r_i & -c) == (c_i & -c)
    tri = jnp.where(same_chunk & (r_i >= c_i), 1.0, 0.0).astype(BF)
    causal = lax.broadcasted_iota(jnp.int32, (c, c), 0) >= lax.broadcasted_iota(jnp.int32, (c, c), 1)
    lane = lax.broadcasted_iota(jnp.int32, (c, A_QK), 1)
    head_mask = [(lane >= h * A_DK) & (lane < (h + 1) * A_DK) for h in range(A_HEADS)]
    pair = rows >= LANES
    if pair:
        pos = lax.broadcasted_iota(jnp.int32, (A_DK, LANES), 1)
        pos_mask = [pos < c, pos >= c]
    tail = 8 if group * 8 >= 32 else c
    gn = gn_ref[...]

    def sub_rows(t):
        start = t * rows
        return start if isinstance(start, int) else pl.multiple_of(start, rows)

    def step(te, se, tp, sp):
        if tp is not None:
            rp = pl.ds(sub_rows(tp), rows)
            la = la_ref[rp, :]
            la_hi = la.astype(BF)
            rem = la - la_hi.astype(F32)
            la_mid = rem.astype(BF)
            la_lo = (rem - la_mid.astype(F32)).astype(BF)
            b3 = _dot(tri, jnp.concatenate([la_hi, la_mid, la_lo], axis=1))
        if te is not None:
            row0 = sub_rows(te)
            re = pl.ds(row0, rows)
            states = [s_ref[...]]
            for ci in range(group):
                states.append(dec_ref[se, ci] * states[-1] + inc_ref[se, ci])
            s_ref[...] = states[-1]
            q_bd = [qbd_ref[se, ci] for ci in range(group)]
            scores = [_dot_nt(q_bd[ci], kdec_ref[se, ci * c:(ci + 1) * c, :]) for ci in range(group)]
            o_inter = [_dot(q_bd[ci], states[ci].astype(BF)) for ci in range(group)]
        if tp is not None:
            b = b3[:, :A_QK] + b3[:, A_QK:2 * A_QK] + b3[:, 2 * A_QK:]
            k = k_ref[rp, :]
            q_dec = (q_ref[rp, :] * (A_DK ** -0.5) * jnp.exp(b)).astype(BF)
            kdec_ref[sp] = (k * jnp.exp(-b)).astype(BF)
            b_last = jnp.concatenate(
                [jnp.broadcast_to(b[(ci + 1) * c - 1:(ci + 1) * c, :], (c, A_QK)) for ci in range(group)], axis=0)
            ku_t = (k * jnp.exp(b_last - b)).T.astype(BF)
            tail_t = jnp.concatenate([b[(ci + 1) * c - tail:(ci + 1) * c, :] for ci in range(group)], axis=0).T
            for ci in range(group):
                q_c = q_dec[ci * c:(ci + 1) * c, :]
                qbd_ref[sp, ci] = jnp.concatenate(
                    [jnp.where(head_mask[h], q_c, jnp.zeros_like(q_c)) for h in range(A_HEADS)], axis=0)
                decay = jnp.exp(tail_t[:, (ci + 1) * tail - 1:(ci + 1) * tail])
                dec_ref[sp, ci] = jnp.broadcast_to(decay, (A_QK, A_DV))
        if te is not None:
            v = v_ref[re, :]
            r = r_ref[re, :]
            for ci in range(group):
                cr = slice(ci * c, (ci + 1) * c)
                outs = []
                for h in range(A_HEADS):
                    hs = slice(h * c, (h + 1) * c)
                    hv = slice(h * A_DV, (h + 1) * A_DV)
                    s_h = jnp.where(causal, scores[ci][hs, :], 0.0).astype(BF)
                    o = _dot(s_h, v[cr, hv]) + o_inter[ci][hs, :]
                    outs.append((_rmsnorm(o, gn) * _silu(r[cr, hv])).astype(BF))
                g_ref[pl.ds(row0 + ci * c, c), :] = jnp.concatenate(outs, axis=1)
        if tp is not None:
            v = v_ref[rp, :]
            for ci in range(group):
                incs = []
                for h in range(A_HEADS):
                    hk = slice(h * A_DK, (h + 1) * A_DK)
                    hv = slice(h * A_DV, (h + 1) * A_DV)
                    if pair:
                        p0 = (ci // 2) * LANES
                        lhs = ku_t[hk, p0:p0 + LANES]
                        lhs = jnp.where(pos_mask[ci % 2], lhs, jnp.zeros_like(lhs))
                        incs.append(_dot(lhs, v[p0:p0 + LANES, hv]))
                    else:
                        incs.append(_dot(ku_t[hk, :], v[:, hv]))
                inc_ref[sp, ci] = jnp.concatenate(incs, axis=0)

    s_ref[...] = s0_ref[...]
    step(None, None, 0, 0)
    if n_sub > 1:
        def body(j, carry):
            t = 2 * j
            step(t, 0, t + 1, 1)
            step(t + 1, 1, t + 2, 0)
            return carry

        lax.fori_loop(0, n_sub // 2 - 1, body, 0)
        step(n_sub - 2, 0, n_sub - 1, 1)
        step(n_sub - 1, 1, None, None)
    else:
        step(0, 0, None, None)
    sout_ref[...] = s_ref[...]


def _gla_call(aq, ak, av, la, ar, s0, gn, *, row0, seq, chunk, group, name):
    n_streams = s0.shape[0]
    rows = chunk * group
    n_sub = seq // rows
    assert seq % rows == 0 and (n_sub == 1 or n_sub % 2 == 0) and chunk & (chunk - 1) == 0
    base = row0 // seq
    tok = lambda w: pl.BlockSpec((seq, w), lambda b: (base + b, 0))
    st = pl.BlockSpec((None, A_QK, A_DV), lambda b: (b, 0, 0))
    return pl.pallas_call(
        functools.partial(_gla_kernel, chunk=chunk, group=group, n_sub=n_sub),
        grid=(n_streams,),
        in_specs=[tok(A_QK), tok(A_QK), tok(A_VW), tok(A_QK), tok(A_VW), st,
                  pl.BlockSpec((1, A_DV), lambda b: (0, 0))],
        out_specs=[pl.BlockSpec((seq, A_VW), lambda b: (b, 0)), st],
        out_shape=[jax.ShapeDtypeStruct((n_streams * seq, A_VW), BF),
                   jax.ShapeDtypeStruct((n_streams, A_QK, A_DV), F32)],
        scratch_shapes=[pltpu.VMEM((2, group, A_HEADS * chunk, A_QK), BF), pltpu.VMEM((2, rows, A_QK), BF),
                        pltpu.VMEM((2, group, A_QK, A_DV), F32), pltpu.VMEM((2, group, A_QK, A_DV), F32),
                        pltpu.VMEM((A_QK, A_DV), F32)],
        compiler_params=pltpu.CompilerParams(dimension_semantics=("arbitrary",), vmem_limit_bytes=VMEM_LIMIT),
        name=name,
    )(aq, ak, av, la, ar, s0, gn)


def _bias_block(brow_ref, blk, tq, tk):
    rows = []
    for h in range(4):
        f = jnp.broadcast_to(brow_ref[blk * 4 + h:blk * 4 + h + 1, :], (tq, BIAS_ROW))
        rows.append(pltpu.roll(f, BIAS_ROW - CHUNK, 1, stride=1, stride_axis=0)[:, :tk])
    return jnp.concatenate(rows, axis=0)


def _block_diag_rows(qb):
    lane = lax.broadcasted_iota(jnp.int32, qb.shape, 1)
    return jnp.concatenate(
        [jnp.where((lane >= h * B_HD) & (lane < (h + 1) * B_HD), qb, jnp.zeros_like(qb)) for h in range(4)],
        axis=0)


def _attn_scores(qb, kw, valid):
    s = _dot_nt(_block_diag_rows(qb), kw)
    return s if valid is None else jnp.where(valid, s, -1e30)


def _attn_softmax(s, bias_tail):
    s = jnp.concatenate([s[:, :BIAS_FLAT], s[:, BIAS_FLAT:] + bias_tail], axis=1)
    p = jnp.exp2(s - jnp.max(s, axis=-1, keepdims=True))
    l = jnp.sum(p, axis=-1, keepdims=True)
    return p.astype(BF), jnp.broadcast_to(l, (s.shape[0], LANES))


def _attn_norm(o, l, gain, diag, tq):
    o = jnp.where(diag, o, 0.0)
    inv = 1.0 / l
    ms = jnp.sum(o * o, axis=-1, keepdims=True) * (inv * inv * (1.0 / B_HD))
    f = inv * lax.rsqrt(ms + EPS)
    y = o * jnp.concatenate([f, f], axis=1)
    return (y[0:tq] + y[tq:2 * tq] + y[2 * tq:3 * tq] + y[3 * tq:4 * tq]) * gain


def _diag_mask(tq):
    r = lax.broadcasted_iota(jnp.int32, (4 * tq, 4 * B_HD), 0)
    cidx = lax.broadcasted_iota(jnp.int32, (4 * tq, 4 * B_HD), 1)
    m = None
    for h in range(4):
        t = (r >= h * tq) & (r < (h + 1) * tq) & (cidx >= h * B_HD) & (cidx < (h + 1) * B_HD)
        m = t if m is None else (m | t)
    return m


def _attn_prompt_kernel(q_ref, k_ref, v_ref, brow_ref, gain_ref, o_ref,
                        kpad_ref, vpad_ref, bias_ref, s_ref, p_ref, l_ref, acc_ref, *, seq):
    tk = B_LEFT + CHUNK
    n_chunks = seq // CHUNK
    kpad_ref[0:B_LEFT, :] = jnp.zeros((B_LEFT, B_W), BF)
    vpad_ref[0:B_LEFT, :] = jnp.zeros((B_LEFT, B_W), BF)
    kpad_ref[B_LEFT:, :] = k_ref[...]
    vpad_ref[B_LEFT:, :] = v_ref[...]
    for blk in range(2):
        bias_ref[blk] = _bias_block(brow_ref, blk, CHUNK, tk)[:, BIAS_FLAT:]
    gain = gain_ref[...]
    diag = _diag_mask(CHUNK)
    key_pos = lax.broadcasted_iota(jnp.int32, (1, tk), 1)
    blocks = [slice(blk * 4 * B_HD, (blk + 1) * 4 * B_HD) for blk in range(2)]

    def scores_stage(ci, slot, masked, blk):
        r0 = pl.multiple_of(ci * CHUNK, CHUNK)
        valid = (key_pos + r0 >= B_LEFT) if masked else None
        cs = blocks[blk]
        s_ref[slot, blk] = _attn_scores(q_ref[pl.ds(r0, CHUNK), cs], kpad_ref[pl.ds(r0, tk), cs], valid)

    def softmax_stage(slot, blk):
        p, l = _attn_softmax(s_ref[slot, blk], bias_ref[blk])
        p_ref[slot, blk] = p
        l_ref[slot, blk] = l

    def values_stage(ci, slot, blk):
        r0 = pl.multiple_of(ci * CHUNK, CHUNK)
        acc_ref[slot, blk] = _dot(p_ref[slot, blk], vpad_ref[pl.ds(r0, tk), blocks[blk]])

    def norm_stage(ci, slot, blk):
        r0 = pl.multiple_of(ci * CHUNK, CHUNK)
        o_ref[pl.ds(r0, CHUNK), blocks[blk]] = _attn_norm(
            acc_ref[slot, blk], l_ref[slot, blk], gain, diag, CHUNK).astype(BF)

    def step(i, parity, masked, live):
        for blk in range(2):
            if live[0]:
                scores_stage(i, parity, masked, blk)
            if live[3]:
                norm_stage(i - 3, 1 - parity, blk)
            if live[1]:
                softmax_stage(1 - parity, blk)
            if live[2]:
                values_stage(i - 2, parity, blk)

    def static_step(i, masked):
        step(i, i % 2, masked, [0 <= i - k < n_chunks for k in range(4)])

    def pair(masked):
        def body(j, carry):
            for d in range(2):
                step(2 * j + d, d, masked, [True] * 4)
            return carry
        return body

    n_masked = B_LEFT // CHUNK
    for i in range(4):
        static_step(i, True)
    lax.fori_loop(2, n_masked // 2, pair(True), 0)
    lax.fori_loop(n_masked // 2, n_chunks // 2, pair(False), 0)
    for i in range(n_chunks, n_chunks + 3):
        static_step(i, False)


def _attn_prompt_call(bq, bk, bv, brow, gain, *, n_streams, seq):
    tok = pl.BlockSpec((seq, B_W), lambda b: (b, 0))
    return pl.pallas_call(
        functools.partial(_attn_prompt_kernel, seq=seq),
        grid=(n_streams,),
        in_specs=[tok, tok, tok, pl.BlockSpec(brow.shape, lambda b: (0, 0)),
                  pl.BlockSpec(gain.shape, lambda b: (0, 0))],
        out_specs=tok,
        out_shape=jax.ShapeDtypeStruct((n_streams * seq, B_W), BF),
        scratch_shapes=[pltpu.VMEM((B_LEFT + seq, B_W), BF), pltpu.VMEM((B_LEFT + seq, B_W), BF),
                        pltpu.VMEM((2, 4 * CHUNK, B_LEFT + CHUNK - BIAS_FLAT), F32),
                        pltpu.VMEM((2, 2, 4 * CHUNK, B_LEFT + CHUNK), F32),
                        pltpu.VMEM((2, 2, 4 * CHUNK, B_LEFT + CHUNK), BF),
                        pltpu.VMEM((2, 2, 4 * CHUNK, LANES), F32),
                        pltpu.VMEM((2, 2, 4 * CHUNK, 4 * B_HD), F32)],
        compiler_params=pltpu.CompilerParams(dimension_semantics=("arbitrary",), vmem_limit_bytes=VMEM_LIMIT),
        name="attn_prompt",
    )(bq, bk, bv, brow, gain)


def _attn_sample_kernel(q_ref, k_ref, v_ref, ck_ref, cv_ref, brow_ref, gain_ref, o_ref, *, tq):
    w = ck_ref.shape[-1]
    q = q_ref[...].astype(F32)
    k_new = k_ref[...].astype(F32)
    v_new = v_ref[...].astype(F32)
    gain = gain_ref[:, :B_HD]
    outs = []
    for h in range(B_HEADS):
        cols = slice(h * B_HD, (h + 1) * B_HD)
        q_h = q[:, cols].astype(BF)
        f = jnp.broadcast_to(brow_ref[h:h + 1, :], (tq, BIAS_ROW))
        bias = pltpu.roll(f, BIAS_ROW - CHUNK, 1, stride=1, stride_axis=0)[:, :w + tq]
        s = jnp.concatenate([_dot(q_h, ck_ref[h].astype(BF)), _dot_nt(q_h, k_new[:, cols].astype(BF))], axis=1) + bias
        p = jnp.exp2(s - jnp.max(s, axis=-1, keepdims=True))
        l = jnp.sum(p, axis=-1, keepdims=True)
        p = p.astype(BF)
        o = (_dot_nt(p[:, :w], cv_ref[h].astype(BF)) + _dot(p[:, w:], v_new[:, cols].astype(BF))) / l
        outs.append(_rmsnorm(o, gain))
    o_ref[...] = jnp.concatenate(outs, axis=1).astype(BF)


def _attn_sample_call(bq, bk, bv, cache_k, cache_v, brow, gain, *, layer, row0, tq):
    n_streams, w = cache_k.shape[1], cache_k.shape[-1]
    base = row0 // tq
    tok = pl.BlockSpec((tq, B_W), lambda b: (base + b, 0))
    cache = pl.BlockSpec((None, None, B_HEADS, B_HD, w), lambda b: (layer, b, 0, 0, 0))
    return pl.pallas_call(
        functools.partial(_attn_sample_kernel, tq=tq),
        grid=(n_streams,),
        in_specs=[tok, tok, tok, cache, cache, pl.BlockSpec(brow.shape, lambda b: (0, 0)),
                  pl.BlockSpec(gain.shape, lambda b: (0, 0))],
        out_specs=pl.BlockSpec((tq, B_W), lambda b: (b, 0)),
        out_shape=jax.ShapeDtypeStruct((n_streams * tq, B_W), BF),
        compiler_params=pltpu.CompilerParams(dimension_semantics=("arbitrary",), vmem_limit_bytes=VMEM_LIMIT),
        name="attn_sample",
    )(bq, bk, bv, cache_k, cache_v, brow, gain)


def kernel(x_prompt, x_sample, state_gla, cache_k, cache_v, norm_ffn1, w_ffn1_gate, w_ffn1_up, w_ffn1_down,
           norm_mix, w_in, w_alpha, b_alpha, gla_norm, attn_bias, attn_norm, w_out, norm_ffn2, w_ffn2_gate,
           w_ffn2_up, w_ffn2_down, final_norm):
    batch, seq, _ = x_prompt.shape
    dec_batch, dec_seq, _ = x_sample.shape
    n_prompt = batch * seq
    n_sample = dec_batch * dec_seq
    past = cache_k.shape[2]
    assert seq % TM == 0 and n_sample % TM == 0 and TM == B_LEFT and past == B_LEFT
    assert dec_seq <= CHUNK
    assert seq % (2 * CHUNK) == 0 and seq >= B_LEFT + 2 * CHUNK

    x = (x_prompt.reshape(n_prompt, D_MODEL), x_sample.reshape(n_sample, D_MODEL))
    row = lambda v: v.reshape(1, -1).astype(F32)
    sizes = dict(n_prompt=n_prompt, n_sample=n_sample)
    cache_t = lambda c: jnp.transpose(c, (0, 1, 3, 4, 2))
    ck_t, cv_t = cache_t(cache_k), cache_t(cache_v)

    o_gate = 2 * A_QK + A_VW
    o_r = o_gate + A_RANK
    o_b = o_r + A_VW

    gla_p, gla_s, proj_in = [], [], []
    zero_state = jnp.zeros((batch, A_QK, A_DV), F32)
    for l in range(DEPTH):
        bf = lambda w: w.astype(BF)
        x = _ffn_call(x, row(norm_ffn1[l]), bf(w_ffn1_gate[l]), bf(w_ffn1_up[l]), bf(w_ffn1_down[l]), **sizes)

        wl = w_in[l]
        wgate = jnp.pad(wl[:, o_gate:o_r], ((0, 0), (0, LANES - A_RANK)))
        walpha = jnp.pad(w_alpha[l], ((0, LANES - A_RANK), (0, 0)))
        proj_in.append(x)
        aq, ak, av, la, ar, bq, bk, bv = _inproj_call(
            x, row(norm_mix[l]), bf(wl[:, :o_gate]), bf(wgate), bf(wl[:, o_r:o_b]), bf(wl[:, o_b:]),
            bf(walpha), row(b_alpha[l]))

        gn = row(gla_norm[l])
        g_p, sp = _gla_call(aq, ak, av, la, ar, zero_state, gn, row0=0, seq=seq, chunk=CHUNK,
                            group=GLA_GROUP, name="gla_prompt")
        g_s, ss = _gla_call(aq, ak, av, la, ar, state_gla[l].reshape(dec_batch, A_QK, A_DV), gn,
                            row0=n_prompt, seq=dec_seq, chunk=dec_seq, group=1, name="gla_sample")

        brow = jnp.pad(attn_bias[l][:, ::-1],
                       ((0, 0), (B_LEFT - CHUNK, BIAS_ROW - (B_LEFT - CHUNK) - (2 * REL_CLIP + 1))), mode="edge")
        brow = (brow - brow[:, :1]) * LOG2E
        gain = jnp.tile(row(attn_norm[l]), (1, 4))
        a_p = _attn_prompt_call(bq, bk, bv, brow, gain, n_streams=batch, seq=seq)
        a_s = _attn_sample_call(bq, bk, bv, ck_t, cv_t, brow, gain,
                                layer=l, row0=n_prompt, tq=dec_seq)

        wo = bf(w_out[l])
        x = _ffn_call(x, row(norm_ffn2[l]), bf(w_ffn2_gate[l]), bf(w_ffn2_up[l]), bf(w_ffn2_down[l]),
                      mix=((g_p, g_s), (a_p, a_s), wo[:A_VW], wo[A_VW:]),
                      final_norm=row(final_norm) if l == DEPTH - 1 else None, **sizes)
        gla_p.append(sp.reshape(batch, A_HEADS, A_DK, A_DV))
        gla_s.append(ss.reshape(dec_batch, A_HEADS, A_DK, A_DV))

    norms = norm_mix.reshape(DEPTH, 1, D_MODEL).astype(F32)
    w_kv = w_in[:, :, o_b + B_W:]
    tiles_per_stream = seq // TM
    k_p, v_p = _kv_tails_call(proj_in, norms, jnp.swapaxes(w_kv, 1, 2).astype(BF), first_tile=tiles_per_stream - 1,
                              tile_step=tiles_per_stream, n_tiles=batch, transposed=True)
    k_s, v_s = _kv_tails_call(proj_in, norms, w_kv.astype(BF), first_tile=n_prompt // TM, tile_step=1,
                              n_tiles=n_sample // TM, transposed=False)
    tail_rows = lambda t: jnp.transpose(t, (0, 1, 4, 2, 3))
    y_prompt, y_sample = x
    return (y_prompt.reshape(batch, seq, D_MODEL), y_sample.reshape(dec_batch, dec_seq, D_MODEL),
            jnp.stack(gla_p), tail_rows(k_p), tail_rows(v_p), jnp.stack(gla_s),
            k_s.reshape(DEPTH, dec_batch, dec_seq, B_HEADS, B_HD),
            v_s.reshape(DEPTH, dec_batch, dec_seq, B_HEADS, B_HD))
```

```python
import functools

import jax
import jax.numpy as jnp
from jax import lax
from jax.experimental import pallas as pl
from jax.experimental.pallas import tpu as pltpu

F32 = jnp.float32
BF = jnp.bfloat16

D_MODEL = 1024
D_FF = 2816
DEPTH = 2
CHUNK = 64
A_HEADS = 4
A_DK = 64
A_DV = 128
A_QK = A_HEADS * A_DK
A_VW = A_HEADS * A_DV
A_RANK = 16
A_TAU = 16.0
B_HEADS = 8
B_HD = 64
B_W = B_HEADS * B_HD
B_LEFT = 8 * CHUNK
REL_CLIP = 128
EPS = 1e-6

TM = 512
FF_CHUNK = 256
GLA_GROUP = 4
LANES = 128
BIAS_ROW = 768
BIAS_FLAT = B_LEFT - REL_CLIP
VMEM_LIMIT = 56 * 1024 * 1024
LOG2E = 1.4426950408889634
Q_SCALE = B_HD ** -0.5 * LOG2E


def _rmsnorm(x, g):
    ms = jnp.mean(x * x, axis=-1, keepdims=True)
    return x * lax.rsqrt(ms + EPS) * g


def _rms_parts(x, g):
    scale = lax.rsqrt(jnp.mean(x * x, axis=-1, keepdims=True) + EPS)
    return (x * g).astype(BF), scale


def _silu(x):
    return x * jax.nn.sigmoid(x)


def _dot(a, b):
    return jnp.dot(a, b, preferred_element_type=F32)


def _dot_nt(a, b):
    return lax.dot_general(a, b, (((1,), (1,)), ((), ())), preferred_element_type=F32)


def _const_spec(shape):
    return pl.BlockSpec(shape, lambda *_: (0,) * len(shape), pipeline_mode=pl.Buffered(1))


def _ffn_kernel(*refs, first, mix, final, prompt_tiles):
    refs = list(refs)
    i = pl.program_id(0)
    if first:
        xp_ref, xs_ref = refs[:2]
        refs = refs[2:]
        x = jnp.where(i < prompt_tiles, xp_ref[...], xs_ref[...])
    else:
        x = refs.pop(0)[...]
    if mix:
        gp_ref, gs_ref, ap_ref, as_ref, wog_ref, woa_ref = refs[:6]
        refs = refs[6:]
        g = jnp.where(i < prompt_tiles, gp_ref[...], gs_ref[...])
        a = jnp.where(i < prompt_tiles, ap_ref[...], as_ref[...])
        x = x + _dot(g, wog_ref[...]) + _dot(a, woa_ref[...])
    n_ref, wg_ref, wu_ref, wd_ref = refs[:4]
    refs = refs[4:]
    if final:
        fn_ref = refs.pop(0)
    act_ref = refs.pop()

    h, scale = _rms_parts(x, n_ref[...])
    for c in range(D_FF // FF_CHUNK):
        sl = slice(c * FF_CHUNK, (c + 1) * FF_CHUNK)
        g = _dot(h, wg_ref[:, sl]) * scale
        u = _dot(h, wu_ref[:, sl]) * scale
        act_ref[:, sl] = (_silu(g) * u).astype(BF)
    y = x + 0.5 * _dot(act_ref[...], wd_ref[...])
    if not final:
        refs[0][...] = y
        return
    y = _rmsnorm(y, fn_ref[...])
    yp_ref, ys_ref = refs

    @pl.when(i < prompt_tiles)
    def _():
        yp_ref[...] = y

    @pl.when(i >= prompt_tiles)
    def _():
        ys_ref[...] = y


def _split_specs(width, prompt_tiles):
    return [pl.BlockSpec((TM, width), lambda i: (jnp.minimum(i, prompt_tiles - 1), 0)),
            pl.BlockSpec((TM, width), lambda i: (jnp.maximum(i - prompt_tiles, 0), 0))]


def _ffn_call(x, norm, wg, wu, wd, *, n_prompt, n_sample, mix=None, final_norm=None):
    n = n_prompt + n_sample
    prompt_tiles = n_prompt // TM
    first = isinstance(x, tuple)
    tok = lambda w: pl.BlockSpec((TM, w), lambda i: (i, 0))
    args = list(x) if first else [x]
    specs = _split_specs(D_MODEL, prompt_tiles) if first else [tok(D_MODEL)]
    if mix is not None:
        g, a, wog, woa = mix
        args += [*g, *a, wog, woa]
        specs += _split_specs(A_VW, prompt_tiles) + _split_specs(B_W, prompt_tiles)
        specs += [_const_spec((A_VW, D_MODEL)), _const_spec((B_W, D_MODEL))]
    args += [norm, wg, wu, wd]
    specs += [_const_spec((1, D_MODEL)), _const_spec((D_MODEL, D_FF)), _const_spec((D_MODEL, D_FF)),
              _const_spec((D_FF, D_MODEL))]
    if final_norm is not None:
        args.append(final_norm)
        specs.append(_const_spec((1, D_MODEL)))
        out_specs = _split_specs(D_MODEL, prompt_tiles)
        out_shape = [jax.ShapeDtypeStruct((n_prompt, D_MODEL), F32), jax.ShapeDtypeStruct((n_sample, D_MODEL), F32)]
    else:
        out_specs = tok(D_MODEL)
        out_shape = jax.ShapeDtypeStruct((n, D_MODEL), F32)
    return pl.pallas_call(
        functools.partial(_ffn_kernel, first=first, mix=mix is not None, final=final_norm is not None,
                          prompt_tiles=prompt_tiles),
        grid=(n // TM,),
        in_specs=specs,
        out_specs=out_specs,
        out_shape=out_shape,
        scratch_shapes=[pltpu.VMEM((TM, D_FF), BF)],
        compiler_params=pltpu.CompilerParams(dimension_semantics=("arbitrary",), vmem_limit_bytes=VMEM_LIMIT),
        name="ffn_mix" if mix is not None else "ffn",
    )(*args)


def _inproj_kernel(x_ref, n_ref, wa_ref, wgate_ref, wr_ref, wb_ref, walpha_ref, balpha_ref,
                   aq_ref, ak_ref, av_ref, la_ref, ar_ref, bq_ref, bk_ref, bv_ref):
    h, scale = _rms_parts(x_ref[...], n_ref[...])
    ag = _dot(h, wgate_ref[...]) * scale
    za = _dot(h, wa_ref[...]) * scale
    pre = _dot(ag.astype(BF), walpha_ref[...]) + balpha_ref[...]
    aq_ref[...] = za[:, :A_QK]
    ak_ref[...] = za[:, A_QK:2 * A_QK]
    av_ref[...] = za[:, 2 * A_QK:].astype(BF)
    ar_ref[...] = _dot(h, wr_ref[...]) * scale
    log_sig = jnp.minimum(pre, 0.0) - jnp.log1p(jnp.exp(-jnp.abs(pre)))
    la_ref[...] = log_sig * (1.0 / A_TAU)
    zb = _dot(h, wb_ref[...]) * scale
    bq_ref[...] = (zb[:, :B_W] * Q_SCALE).astype(BF)
    bk_ref[...] = zb[:, B_W:2 * B_W].astype(BF)
    bv_ref[...] = zb[:, 2 * B_W:].astype(BF)


def _inproj_call(x, norm, wa, wgate, wr, wb, walpha, balpha):
    n = x.shape[0]
    tok = lambda w: pl.BlockSpec((TM, w), lambda i: (i, 0))
    outs = [
        (A_QK, F32), (A_QK, F32), (A_VW, BF), (A_QK, F32), (A_VW, F32), (B_W, BF), (B_W, BF), (B_W, BF)]
    return pl.pallas_call(
        _inproj_kernel,
        grid=(n // TM,),
        in_specs=[tok(D_MODEL), _const_spec((1, D_MODEL)), _const_spec(wa.shape), _const_spec(wgate.shape),
                  _const_spec(wr.shape), _const_spec(wb.shape), _const_spec(walpha.shape),
                  _const_spec(balpha.shape)],
        out_specs=[tok(w) for w, _ in outs],
        out_shape=[jax.ShapeDtypeStruct((n, w), dt) for w, dt in outs],
        compiler_params=pltpu.CompilerParams(dimension_semantics=("arbitrary",), vmem_limit_bytes=VMEM_LIMIT),
        name="inproj",
    )(x, norm, wa, wgate, wr, wb, walpha, balpha)


def _kv_tails_kernel(*refs, transposed):
    x_refs = refs[:DEPTH]
    n_ref, w_ref, k_ref, v_ref = refs[DEPTH:]
    layer = pl.program_id(0)
    x = x_refs[0][...]
    for l in range(1, DEPTH):
        x = jnp.where(layer == l, x_refs[l][...], x)
    h = _rmsnorm(x, n_ref[...]).astype(BF)
    if transposed:
        kv = _dot_nt(w_ref[...], h)
        k_ref[...] = kv[:B_W].reshape(B_HEADS, B_HD, TM)
        v_ref[...] = kv[B_W:].reshape(B_HEADS, B_HD, TM)
    else:
        kv = _dot(h, w_ref[...])
        k_ref[...] = kv[:, :B_W]
        v_ref[...] = kv[:, B_W:]


def _kv_tails_call(xs, norms, w_kv, *, first_tile, tile_step, n_tiles, transposed):
    tok = [pl.BlockSpec((TM, D_MODEL),
                        lambda l, t, own=own: (jnp.where(l == own, first_tile + t * tile_step, first_tile), 0))
           for own in range(DEPTH)]
    if transposed:
        out = pl.BlockSpec((None, None, B_HEADS, B_HD, TM), lambda l, t: (l, t, 0, 0, 0))
        shape = jax.ShapeDtypeStruct((DEPTH, n_tiles, B_HEADS, B_HD, TM), F32)
    else:
        out = pl.BlockSpec((None, TM, B_W), lambda l, t: (l, t, 0))
        shape = jax.ShapeDtypeStruct((DEPTH, n_tiles * TM, B_W), F32)
    return pl.pallas_call(
        functools.partial(_kv_tails_kernel, transposed=transposed),
        grid=(DEPTH, n_tiles),
        in_specs=tok + [pl.BlockSpec((None, 1, D_MODEL), lambda l, t: (l, 0, 0)),
                        pl.BlockSpec((None,) + w_kv.shape[1:], lambda l, t: (l, 0, 0))],
        out_specs=[out, out],
        out_shape=[shape, shape],
        compiler_params=pltpu.CompilerParams(dimension_semantics=("arbitrary", "arbitrary"),
                                             vmem_limit_bytes=VMEM_LIMIT),
        name="kv_tails_prompt" if transposed else "kv_tails_sample",
    )(*xs, norms, w_kv)


def _gla_kernel(q_ref, k_ref, v_ref, la_ref, r_ref, s0_ref, gn_ref, g_ref, sout_ref,
                qbd_ref, kdec_ref, inc_ref, dec_ref, s_ref, *, chunk, group, n_sub):
    c = chunk
    rows = group * c
    r_i = lax.broadcasted_iota(jnp.int32, (rows, rows), 0)
    c_i = lax.broadcasted_iota(jnp.int32, (rows, rows), 1)
    same_chunk = (r_i & -c) == (c_i & -c)
    tri = jnp.where(same_chunk & (r_i >= c_i), 1.0, 0.0).astype(BF)
    causal = lax.broadcasted_iota(jnp.int32, (c, c), 0) >= lax.broadcasted_iota(jnp.int32, (c, c), 1)
    lane = lax.broadcasted_iota(jnp.int32, (c, A_QK), 1)
    head_mask = [(lane >= h * A_DK) & (lane < (h + 1) * A_DK) for h in range(A_HEADS)]
    pair = rows >= LANES
    if pair:
        pos = lax.broadcasted_iota(jnp.int32, (A_DK, LANES), 1)
        pos_mask = [pos < c, pos >= c]
    tail = 8 if group * 8 >= 32 else c
    gn = gn_ref[...]

    def sub_rows(t):
        start = t * rows
        return start if isinstance(start, int) else pl.multiple_of(start, rows)

    def step(te, se, tp, sp):
        if tp is not None:
            rp = pl.ds(sub_rows(tp), rows)
            la = la_ref[rp, :]
            la_hi = la.astype(BF)
            rem = la - la_hi.astype(F32)
            la_mid = rem.astype(BF)
            la_lo = (rem - la_mid.astype(F32)).astype(BF)
            b3 = _dot(tri, jnp.concatenate([la_hi, la_mid, la_lo], axis=1))
        if te is not None:
            row0 = sub_rows(te)
            re = pl.ds(row0, rows)
            states = [s_ref[...]]
            for ci in range(group):
                states.append(dec_ref[se, ci] * states[-1] + inc_ref[se, ci])
            s_ref[...] = states[-1]
            q_bd = [qbd_ref[se, ci] for ci in range(group)]
            scores = [_dot_nt(q_bd[ci], kdec_ref[se, ci * c:(ci + 1) * c, :]) for ci in range(group)]
            o_inter = [_dot(q_bd[ci], states[ci].astype(BF)) for ci in range(group)]
        if tp is not None:
            b = b3[:, :A_QK] + b3[:, A_QK:2 * A_QK] + b3[:, 2 * A_QK:]
            k = k_ref[rp, :]
            q_dec = (q_ref[rp, :] * (A_DK ** -0.5) * jnp.exp(b)).astype(BF)
            kdec_ref[sp] = (k * jnp.exp(-b)).astype(BF)
            b_last = jnp.concatenate(
                [jnp.broadcast_to(b[(ci + 1) * c - 1:(ci + 1) * c, :], (c, A_QK)) for ci in range(group)], axis=0)
            ku_t = (k * jnp.exp(b_last - b)).T.astype(BF)
            tail_t = jnp.concatenate([b[(ci + 1) * c - tail:(ci + 1) * c, :] for ci in range(group)], axis=0).T
            for ci in range(group):
                q_c = q_dec[ci * c:(ci + 1) * c, :]
                qbd_ref[sp, ci] = jnp.concatenate(
                    [jnp.where(head_mask[h], q_c, jnp.zeros_like(q_c)) for h in range(A_HEADS)], axis=0)
                decay = jnp.exp(tail_t[:, (ci + 1) * tail - 1:(ci + 1) * tail])
                dec_ref[sp, ci] = jnp.broadcast_to(decay, (A_QK, A_DV))
        if te is not None:
            v = v_ref[re, :]
            r = r_ref[re, :]
            for ci in range(group):
                cr = slice(ci * c, (ci + 1) * c)
                outs = []
                for h in range(A_HEADS):
                    hs = slice(h * c, (h + 1) * c)
                    hv = slice(h * A_DV, (h + 1) * A_DV)
                    s_h = jnp.where(causal, scores[ci][hs, :], 0.0).astype(BF)
                    o = _dot(s_h, v[cr, hv]) + o_inter[ci][hs, :]
                    outs.append((_rmsnorm(o, gn) * _silu(r[cr, hv])).astype(BF))
                g_ref[pl.ds(row0 + ci * c, c), :] = jnp.concatenate(outs, axis=1)
        if tp is not None:
            v = v_ref[rp, :]
            for ci in range(group):
                incs = []
                for h in range(A_HEADS):
                    hk = slice(h * A_DK, (h + 1) * A_DK)
                    hv = slice(h * A_DV, (h + 1) * A_DV)
                    if pair:
                        p0 = (ci // 2) * LANES
                        lhs = ku_t[hk, p0:p0 + LANES]
                        lhs = jnp.where(pos_mask[ci % 2], lhs, jnp.zeros_like(lhs))
                        incs.append(_dot(lhs, v[p0:p0 + LANES, hv]))
                    else:
                        incs.append(_dot(ku_t[hk, :], v[:, hv]))
                inc_ref[sp, ci] = jnp.concatenate(incs, axis=0)

    s_ref[...] = s0_ref[...]
    step(None, None, 0, 0)
    if n_sub > 1:
        def body(j, carry):
            t = 2 * j
            step(t, 0, t + 1, 1)
            step(t + 1, 1, t + 2, 0)
            return carry

        lax.fori_loop(0, n_sub // 2 - 1, body, 0)
        step(n_sub - 2, 0, n_sub - 1, 1)
        step(n_sub - 1, 1, None, None)
    else:
        step(0, 0, None, None)
    sout_ref[...] = s_ref[...]


def _gla_call(aq, ak, av, la, ar, s0, gn, *, row0, seq, chunk, group, name):
    n_streams = s0.shape[0]
    rows = chunk * group
    n_sub = seq // rows
    assert seq % rows == 0 and (n_sub == 1 or n_sub % 2 == 0) and chunk & (chunk - 1) == 0
    base = row0 // seq
    tok = lambda w: pl.BlockSpec((seq, w), lambda b: (base + b, 0))
    st = pl.BlockSpec((None, A_QK, A_DV), lambda b: (b, 0, 0))
    return pl.pallas_call(
        functools.partial(_gla_kernel, chunk=chunk, group=group, n_sub=n_sub),
        grid=(n_streams,),
        in_specs=[tok(A_QK), tok(A_QK), tok(A_VW), tok(A_QK), tok(A_VW), st,
                  pl.BlockSpec((1, A_DV), lambda b: (0, 0))],
        out_specs=[pl.BlockSpec((seq, A_VW), lambda b: (b, 0)), st],
        out_shape=[jax.ShapeDtypeStruct((n_streams * seq, A_VW), BF),
                   jax.ShapeDtypeStruct((n_streams, A_QK, A_DV), F32)],
        scratch_shapes=[pltpu.VMEM((2, group, A_HEADS * chunk, A_QK), BF), pltpu.VMEM((2, rows, A_QK), BF),
                        pltpu.VMEM((2, group, A_QK, A_DV), F32), pltpu.VMEM((2, group, A_QK, A_DV), F32),
                        pltpu.VMEM((A_QK, A_DV), F32)],
        compiler_params=pltpu.CompilerParams(dimension_semantics=("arbitrary",), vmem_limit_bytes=VMEM_LIMIT),
        name=name,
    )(aq, ak, av, la, ar, s0, gn)


def _bias_block(brow_ref, blk, tq, tk):
    rows = []
    for h in range(4):
        f = jnp.broadcast_to(brow_ref[blk * 4 + h:blk * 4 + h + 1, :], (tq, BIAS_ROW))
        rows.append(pltpu.roll(f, BIAS_ROW - CHUNK, 1, stride=1, stride_axis=0)[:, :tk])
    return jnp.concatenate(rows, axis=0)


def _block_diag_rows(qb):
    lane = lax.broadcasted_iota(jnp.int32, qb.shape, 1)
    return jnp.concatenate(
        [jnp.where((lane >= h * B_HD) & (lane < (h + 1) * B_HD), qb, jnp.zeros_like(qb)) for h in range(4)],
        axis=0)


def _attn_scores(qb, kw, valid):
    s = _dot_nt(_block_diag_rows(qb), kw)
    return s if valid is None else jnp.where(valid, s, -1e30)


def _attn_softmax(s, bias_tail):
    s = jnp.concatenate([s[:, :BIAS_FLAT], s[:, BIAS_FLAT:] + bias_tail], axis=1)
    p = jnp.exp2(s - jnp.max(s, axis=-1, keepdims=True))
    l = jnp.sum(p, axis=-1, keepdims=True)
    return p.astype(BF), jnp.broadcast_to(l, (s.shape[0], LANES))


def _attn_norm(o, l, gain, diag, tq):
    o = jnp.where(diag, o, 0.0)
    inv = 1.0 / l
    ms = jnp.sum(o * o, axis=-1, keepdims=True) * (inv * inv * (1.0 / B_HD))
    f = inv * lax.rsqrt(ms + EPS)
    y = o * jnp.concatenate([f, f], axis=1)
    return (y[0:tq] + y[tq:2 * tq] + y[2 * tq:3 * tq] + y[3 * tq:4 * tq]) * gain


def _diag_mask(tq):
    r = lax.broadcasted_iota(jnp.int32, (4 * tq, 4 * B_HD), 0)
    cidx = lax.broadcasted_iota(jnp.int32, (4 * tq, 4 * B_HD), 1)
    m = None
    for h in range(4):
        t = (r >= h * tq) & (r < (h + 1) * tq) & (cidx >= h * B_HD) & (cidx < (h + 1) * B_HD)
        m = t if m is None else (m | t)
    return m


def _attn_prompt_kernel(q_ref, k_ref, v_ref, brow_ref, gain_ref, o_ref,
                        kpad_ref, vpad_ref, bias_ref, s_ref, p_ref, l_ref, acc_ref, *, seq):
    tk = B_LEFT + CHUNK
    n_chunks = seq // CHUNK
    kpad_ref[0:B_LEFT, :] = jnp.zeros((B_LEFT, B_W), BF)
    vpad_ref[0:B_LEFT, :] = jnp.zeros((B_LEFT, B_W), BF)
    kpad_ref[B_LEFT:, :] = k_ref[...]
    vpad_ref[B_LEFT:, :] = v_ref[...]
    for blk in range(2):
        bias_ref[blk] = _bias_block(brow_ref, blk, CHUNK, tk)[:, BIAS_FLAT:]
    gain = gain_ref[...]
    diag = _diag_mask(CHUNK)
    key_pos = lax.broadcasted_iota(jnp.int32, (1, tk), 1)
    blocks = [slice(blk * 4 * B_HD, (blk + 1) * 4 * B_HD) for blk in range(2)]

    def scores_stage(ci, slot, masked, blk):
        r0 = pl.multiple_of(ci * CHUNK, CHUNK)
        valid = (key_pos + r0 >= B_LEFT) if masked else None
        cs = blocks[blk]
        s_ref[slot, blk] = _attn_scores(q_ref[pl.ds(r0, CHUNK), cs], kpad_ref[pl.ds(r0, tk), cs], valid)

    def softmax_stage(slot, blk):
        p, l = _attn_softmax(s_ref[slot, blk], bias_ref[blk])
        p_ref[slot, blk] = p
        l_ref[slot, blk] = l

    def values_stage(ci, slot, blk):
        r0 = pl.multiple_of(ci * CHUNK, CHUNK)
        acc_ref[slot, blk] = _dot(p_ref[slot, blk], vpad_ref[pl.ds(r0, tk), blocks[blk]])

    def norm_stage(ci, slot, blk):
        r0 = pl.multiple_of(ci * CHUNK, CHUNK)
        o_ref[pl.ds(r0, CHUNK), blocks[blk]] = _attn_norm(
            acc_ref[slot, blk], l_ref[slot, blk], gain, diag, CHUNK).astype(BF)

    def step(i, parity, masked, live):
        for blk in range(2):
            if live[0]:
                scores_stage(i, parity, masked, blk)
            if live[3]:
                norm_stage(i - 3, 1 - parity, blk)
            if live[1]:
                softmax_stage(1 - parity, blk)
            if live[2]:
                values_stage(i - 2, parity, blk)

    def static_step(i, masked):
        step(i, i % 2, masked, [0 <= i - k < n_chunks for k in range(4)])

    def pair(masked):
        def body(j, carry):
            for d in range(2):
                step(2 * j + d, d, masked, [True] * 4)
            return carry
        return body

    n_masked = B_LEFT // CHUNK
    for i in range(4):
        static_step(i, True)
    lax.fori_loop(2, n_masked // 2, pair(True), 0)
    lax.fori_loop(n_masked // 2, n_chunks // 2, pair(False), 0)
    for i in range(n_chunks, n_chunks + 3):
        static_step(i, False)


def _attn_prompt_call(bq, bk, bv, brow, gain, *, n_streams, seq):
    tok = pl.BlockSpec((seq, B_W), lambda b: (b, 0))
    return pl.pallas_call(
        functools.partial(_attn_prompt_kernel, seq=seq),
        grid=(n_streams,),
        in_specs=[tok, tok, tok, pl.BlockSpec(brow.shape, lambda b: (0, 0)),
                  pl.BlockSpec(gain.shape, lambda b: (0, 0))],
        out_specs=tok,
        out_shape=jax.ShapeDtypeStruct((n_streams * seq, B_W), BF),
        scratch_shapes=[pltpu.VMEM((B_LEFT + seq, B_W), BF), pltpu.VMEM((B_LEFT + seq, B_W), BF),
                        pltpu.VMEM((2, 4 * CHUNK, B_LEFT + CHUNK - BIAS_FLAT), F32),
                        pltpu.VMEM((2, 2, 4 * CHUNK, B_LEFT + CHUNK), F32),
                        pltpu.VMEM((2, 2, 4 * CHUNK, B_LEFT + CHUNK), BF),
                        pltpu.VMEM((2, 2, 4 * CHUNK, LANES), F32),
                        pltpu.VMEM((2, 2, 4 * CHUNK, 4 * B_HD), F32)],
        compiler_params=pltpu.CompilerParams(dimension_semantics=("arbitrary",), vmem_limit_bytes=VMEM_LIMIT),
        name="attn_prompt",
    )(bq, bk, bv, brow, gain)


def _attn_sample_kernel(q_ref, k_ref, v_ref, ck_ref, cv_ref, brow_ref, gain_ref, o_ref, *, tq):
    w = ck_ref.shape[-1]
    q = q_ref[...].astype(F32)
    k_new = k_ref[...].astype(F32)
    v_new = v_ref[...].astype(F32)
    gain = gain_ref[:, :B_HD]
    outs = []
    for h in range(B_HEADS):
        cols = slice(h * B_HD, (h + 1) * B_HD)
        q_h = q[:, cols].astype(BF)
        f = jnp.broadcast_to(brow_ref[h:h + 1, :], (tq, BIAS_ROW))
        bias = pltpu.roll(f, BIAS_ROW - CHUNK, 1, stride=1, stride_axis=0)[:, :w + tq]
        s = jnp.concatenate([_dot(q_h, ck_ref[h].astype(BF)), _dot_nt(q_h, k_new[:, cols].astype(BF))], axis=1) + bias
        p = jnp.exp2(s - jnp.max(s, axis=-1, keepdims=True))
        l = jnp.sum(p, axis=-1, keepdims=True)
        p = p.astype(BF)
        o = (_dot_nt(p[:, :w], cv_ref[h].astype(BF)) + _dot(p[:, w:], v_new[:, cols].astype(BF))) / l
        outs.append(_rmsnorm(o, gain))
    o_ref[...] = jnp.concatenate(outs, axis=1).astype(BF)


def _attn_sample_call(bq, bk, bv, cache_k, cache_v, brow, gain, *, layer, row0, tq):
    n_streams, w = cache_k.shape[1], cache_k.shape[-1]
    base = row0 // tq
    tok = pl.BlockSpec((tq, B_W), lambda b: (base + b, 0))
    cache = pl.BlockSpec((None, None, B_HEADS, B_HD, w), lambda b: (layer, b, 0, 0, 0))
    return pl.pallas_call(
        functools.partial(_attn_sample_kernel, tq=tq),
        grid=(n_streams,),
        in_specs=[tok, tok, tok, cache, cache, pl.BlockSpec(brow.shape, lambda b: (0, 0)),
                  pl.BlockSpec(gain.shape, lambda b: (0, 0))],
        out_specs=pl.BlockSpec((tq, B_W), lambda b: (b, 0)),
        out_shape=jax.ShapeDtypeStruct((n_streams * tq, B_W), BF),
        compiler_params=pltpu.CompilerParams(dimension_semantics=("arbitrary",), vmem_limit_bytes=VMEM_LIMIT),
        name="attn_sample",
    )(bq, bk, bv, cache_k, cache_v, brow, gain)


def kernel(x_prompt, x_sample, state_gla, cache_k, cache_v, norm_ffn1, w_ffn1_gate, w_ffn1_up, w_ffn1_down,
           norm_mix, w_in, w_alpha, b_alpha, gla_norm, attn_bias, attn_norm, w_out, norm_ffn2, w_ffn2_gate,
           w_ffn2_up, w_ffn2_down, final_norm):
    batch, seq, _ = x_prompt.shape
    dec_batch, dec_seq, _ = x_sample.shape
    n_prompt = batch * seq
    n_sample = dec_batch * dec_seq
    past = cache_k.shape[2]
    assert seq % TM == 0 and n_sample % TM == 0 and TM == B_LEFT and past == B_LEFT
    assert dec_seq <= CHUNK
    assert seq % (2 * CHUNK) == 0 and seq >= B_LEFT + 2 * CHUNK

    x = (x_prompt.reshape(n_prompt, D_MODEL), x_sample.reshape(n_sample, D_MODEL))
    row = lambda v: v.reshape(1, -1).astype(F32)
    sizes = dict(n_prompt=n_prompt, n_sample=n_sample)
    cache_t = lambda c: jnp.transpose(c, (0, 1, 3, 4, 2))
    ck_t, cv_t = cache_t(cache_k), cache_t(cache_v)

    o_gate = 2 * A_QK + A_VW
    o_r = o_gate + A_RANK
    o_b = o_r + A_VW

    gla_p, gla_s, proj_in = [], [], []
    zero_state = jnp.zeros((batch, A_QK, A_DV), F32)
    for l in range(DEPTH):
        bf = lambda w: w.astype(BF)
        x = _ffn_call(x, row(norm_ffn1[l]), bf(w_ffn1_gate[l]), bf(w_ffn1_up[l]), bf(w_ffn1_down[l]), **sizes)

        wl = w_in[l]
        wgate = jnp.pad(wl[:, o_gate:o_r], ((0, 0), (0, LANES - A_RANK)))
        walpha = jnp.pad(w_alpha[l], ((0, LANES - A_RANK), (0, 0)))
        proj_in.append(x)
        aq, ak, av, la, ar, bq, bk, bv = _inproj_call(
            x, row(norm_mix[l]), bf(wl[:, :o_gate]), bf(wgate), bf(wl[:, o_r:o_b]), bf(wl[:, o_b:]),
            bf(walpha), row(b_alpha[l]))

        gn = row(gla_norm[l])
        g_p, sp = _gla_call(aq, ak, av, la, ar, zero_state, gn, row0=0, seq=seq, chunk=CHUNK,
                            group=GLA_GROUP, name="gla_prompt")
        g_s, ss = _gla_call(aq, ak, av, la, ar, state_gla[l].reshape(dec_batch, A_QK, A_DV), gn,
                            row0=n_prompt, seq=dec_seq, chunk=dec_seq, group=1, name="gla_sample")

        brow = jnp.pad(attn_bias[l][:, ::-1],
                       ((0, 0), (B_LEFT - CHUNK, BIAS_ROW - (B_LEFT - CHUNK) - (2 * REL_CLIP + 1))), mode="edge")
        brow = (brow - brow[:, :1]) * LOG2E
        gain = jnp.tile(row(attn_norm[l]), (1, 4))
        a_p = _attn_prompt_call(bq, bk, bv, brow, gain, n_streams=batch, seq=seq)
        a_s = _attn_sample_call(bq, bk, bv, ck_t, cv_t, brow, gain,
                                layer=l, row0=n_prompt, tq=dec_seq)

        wo = bf(w_out[l])
        x = _ffn_call(x, row(norm_ffn2[l]), bf(w_ffn2_gate[l]), bf(w_ffn2_up[l]), bf(w_ffn2_down[l]),
                      mix=((g_p, g_s), (a_p, a_s), wo[:A_VW], wo[A_VW:]),
                      final_norm=row(final_norm) if l == DEPTH - 1 else None, **sizes)
        gla_p.append(sp.reshape(batch, A_HEADS, A_DK, A_DV))
        gla_s.append(ss.reshape(dec_batch, A_HEADS, A_DK, A_DV))

    norms = norm_mix.reshape(DEPTH, 1, D_MODEL).astype(F32)
    w_kv = w_in[:, :, o_b + B_W:]
    tiles_per_stream = seq // TM
    k_p, v_p = _kv_tails_call(proj_in, norms, jnp.swapaxes(w_kv, 1, 2).astype(BF), first_tile=tiles_per_stream - 1,
                              tile_step=tiles_per_stream, n_tiles=batch, transposed=True)
    k_s, v_s = _kv_tails_call(proj_in, norms, w_kv.astype(BF), first_tile=n_prompt // TM, tile_step=1,
                              n_tiles=n_sample // TM, transposed=False)
    tail_rows = lambda t: jnp.transpose(t, (0, 1, 4, 2, 3))
    y_prompt, y_sample = x
    return (y_prompt.reshape(batch, seq, D_MODEL), y_sample.reshape(dec_batch, dec_seq, D_MODEL),
            jnp.stack(gla_p), tail_rows(k_p), tail_rows(v_p), jnp.stack(gla_s),
            k_s.reshape(DEPTH, dec_batch, dec_seq, B_HEADS, B_HD),
            v_s.reshape(DEPTH, dec_batch, dec_seq, B_HEADS, B_HD))
```

```python
import functools

import jax
import jax.numpy as jnp
from jax import lax
from jax.experimental import pallas as pl
from jax.experimental.pallas import tpu as pltpu

F32 = jnp.float32
BF = jnp.bfloat16

D_MODEL = 1024
D_FF = 2816
DEPTH = 2
CHUNK = 64
A_HEADS = 4
A_DK = 64
A_DV = 128
A_QK = A_HEADS * A_DK
A_VW = A_HEADS * A_DV
A_RANK = 16
A_TAU = 16.0
B_HEADS = 8
B_HD = 64
B_W = B_HEADS * B_HD
B_LEFT = 8 * CHUNK
REL_CLIP = 128
EPS = 1e-6

TM = 512
FF_CHUNK = 256
GLA_GROUP = 4
LANES = 128
BIAS_ROW = 768
BIAS_FLAT = B_LEFT - REL_CLIP
VMEM_LIMIT = 56 * 1024 * 1024
LOG2E = 1.4426950408889634
Q_SCALE = B_HD ** -0.5 * LOG2E


def _rmsnorm(x, g):
    ms = jnp.mean(x * x, axis=-1, keepdims=True)
    return x * lax.rsqrt(ms + EPS) * g


def _rms_parts(x, g):
    scale = lax.rsqrt(jnp.mean(x * x, axis=-1, keepdims=True) + EPS)
    return (x * g).astype(BF), scale


def _silu(x):
    return x * jax.nn.sigmoid(x)


def _dot(a, b):
    return jnp.dot(a, b, preferred_element_type=F32)


def _dot_nt(a, b):
    return lax.dot_general(a, b, (((1,), (1,)), ((), ())), preferred_element_type=F32)


def _const_spec(shape):
    return pl.BlockSpec(shape, lambda *_: (0,) * len(shape), pipeline_mode=pl.Buffered(1))


def _ffn_kernel(*refs, first, mix, final, prompt_tiles):
    refs = list(refs)
    i = pl.program_id(0)
    if first:
        xp_ref, xs_ref = refs[:2]
        refs = refs[2:]
        x = jnp.where(i < prompt_tiles, xp_ref[...], xs_ref[...])
    else:
        x = refs.pop(0)[...]
    if mix:
        gp_ref, gs_ref, ap_ref, as_ref, wog_ref, woa_ref = refs[:6]
        refs = refs[6:]
        g = jnp.where(i < prompt_tiles, gp_ref[...], gs_ref[...])
        a = jnp.where(i < prompt_tiles, ap_ref[...], as_ref[...])
        x = x + _dot(g, wog_ref[...]) + _dot(a, woa_ref[...])
    n_ref, wg_ref, wu_ref, wd_ref = refs[:4]
    refs = refs[4:]
    if final:
        fn_ref = refs.pop(0)
    act_ref = refs.pop()

    h, scale = _rms_parts(x, n_ref[...])
    for c in range(D_FF // FF_CHUNK):
        sl = slice(c * FF_CHUNK, (c + 1) * FF_CHUNK)
        g = _dot(h, wg_ref[:, sl]) * scale
        u = _dot(h, wu_ref[:, sl]) * scale
        act_ref[:, sl] = (_silu(g) * u).astype(BF)
    y = x + 0.5 * _dot(act_ref[...], wd_ref[...])
    if not final:
        refs[0][...] = y
        return
    y = _rmsnorm(y, fn_ref[...])
    yp_ref, ys_ref = refs

    @pl.when(i < prompt_tiles)
    def _():
        yp_ref[...] = y

    @pl.when(i >= prompt_tiles)
    def _():
        ys_ref[...] = y


def _split_specs(width, prompt_tiles):
    return [pl.BlockSpec((TM, width), lambda i: (jnp.minimum(i, prompt_tiles - 1), 0)),
            pl.BlockSpec((TM, width), lambda i: (jnp.maximum(i - prompt_tiles, 0), 0))]


def _layer_spec(shape, layer, block=0):
    return pl.BlockSpec((None,) + shape, lambda *_: (layer, block) + (0,) * (len(shape) - 1),
                        pipeline_mode=pl.Buffered(1))


def _ffn_call(x, norm, wg, wu, wd, *, layer, n_prompt, n_sample, mix=None, final_norm=None):
    n = n_prompt + n_sample
    prompt_tiles = n_prompt // TM
    first = isinstance(x, tuple)
    tok = lambda w: pl.BlockSpec((TM, w), lambda i: (i, 0))
    args = list(x) if first else [x]
    specs = _split_specs(D_MODEL, prompt_tiles) if first else [tok(D_MODEL)]
    if mix is not None:
        g, a, wo = mix
        args += [*g, *a, wo, wo]
        specs += _split_specs(A_VW, prompt_tiles) + _split_specs(B_W, prompt_tiles)
        specs += [_layer_spec((A_VW, D_MODEL), layer, 0), _layer_spec((B_W, D_MODEL), layer, A_VW // B_W)]
    args += [norm, wg, wu, wd]
    specs += [_const_spec((1, D_MODEL)), _layer_spec((D_MODEL, D_FF), layer), _layer_spec((D_MODEL, D_FF), layer),
              _layer_spec((D_FF, D_MODEL), layer)]
    if final_norm is not None:
        args.append(final_norm)
        specs.append(_const_spec((1, D_MODEL)))
        out_specs = _split_specs(D_MODEL, prompt_tiles)
        out_shape = [jax.ShapeDtypeStruct((n_prompt, D_MODEL), F32), jax.ShapeDtypeStruct((n_sample, D_MODEL), F32)]
    else:
        out_specs = tok(D_MODEL)
        out_shape = jax.ShapeDtypeStruct((n, D_MODEL), F32)
    return pl.pallas_call(
        functools.partial(_ffn_kernel, first=first, mix=mix is not None, final=final_norm is not None,
                          prompt_tiles=prompt_tiles),
        grid=(n // TM,),
        in_specs=specs,
        out_specs=out_specs,
        out_shape=out_shape,
        scratch_shapes=[pltpu.VMEM((TM, D_FF), BF)],
        compiler_params=pltpu.CompilerParams(dimension_semantics=("arbitrary",), vmem_limit_bytes=VMEM_LIMIT),
        name="ffn_mix" if mix is not None else "ffn",
    )(*args)


def _inproj_kernel(x_ref, n_ref, wa_ref, wgate_ref, wr_ref, wb_ref, walpha_ref, balpha_ref,
                   aq_ref, ak_ref, av_ref, la_ref, ar_ref, bq_ref, bk_ref, bv_ref):
    h, scale = _rms_parts(x_ref[...], n_ref[...])
    ag = _dot(h, wgate_ref[...]) * scale
    za = _dot(h, wa_ref[...]) * scale
    pre = _dot(ag.astype(BF), walpha_ref[...]) + balpha_ref[...]
    aq_ref[...] = za[:, :A_QK]
    ak_ref[...] = za[:, A_QK:2 * A_QK]
    av_ref[...] = za[:, 2 * A_QK:].astype(BF)
    ar_ref[...] = _dot(h, wr_ref[...]) * scale
    log_sig = jnp.minimum(pre, 0.0) - jnp.log1p(jnp.exp(-jnp.abs(pre)))
    la_ref[...] = log_sig * (1.0 / A_TAU)
    zb = _dot(h, wb_ref[...]) * scale
    bq_ref[...] = (zb[:, :B_W] * Q_SCALE).astype(BF)
    bk_ref[...] = zb[:, B_W:2 * B_W].astype(BF)
    bv_ref[...] = zb[:, 2 * B_W:].astype(BF)


def _inproj_call(x, norm, wa, wgate, wr, wb, walpha, balpha, *, layer):
    n = x.shape[0]
    tok = lambda w: pl.BlockSpec((TM, w), lambda i: (i, 0))
    outs = [
        (A_QK, F32), (A_QK, F32), (A_VW, BF), (A_QK, F32), (A_VW, F32), (B_W, BF), (B_W, BF), (B_W, BF)]
    return pl.pallas_call(
        _inproj_kernel,
        grid=(n // TM,),
        in_specs=[tok(D_MODEL), _const_spec((1, D_MODEL))]
        + [_layer_spec(w.shape[1:], layer) for w in (wa, wgate, wr, wb, walpha)] + [_const_spec(balpha.shape)],
        out_specs=[tok(w) for w, _ in outs],
        out_shape=[jax.ShapeDtypeStruct((n, w), dt) for w, dt in outs],
        compiler_params=pltpu.CompilerParams(dimension_semantics=("arbitrary",), vmem_limit_bytes=VMEM_LIMIT),
        name="inproj",
    )(x, norm, wa, wgate, wr, wb, walpha, balpha)


def _kv_tails_kernel(*refs, transposed):
    x_refs = refs[:DEPTH]
    n_ref, w_ref, k_ref, v_ref = refs[DEPTH:]
    layer = pl.program_id(0)
    x = x_refs[0][...]
    for l in range(1, DEPTH):
        x = jnp.where(layer == l, x_refs[l][...], x)
    h = _rmsnorm(x, n_ref[...]).astype(BF)
    if transposed:
        kv = _dot_nt(w_ref[...], h)
        k_ref[...] = kv[:B_W].reshape(B_HEADS, B_HD, TM)
        v_ref[...] = kv[B_W:].reshape(B_HEADS, B_HD, TM)
    else:
        kv = _dot(h, w_ref[...])
        k_ref[...] = kv[:, :B_W]
        v_ref[...] = kv[:, B_W:]


def _kv_tails_call(xs, norms, w_kv, *, first_tile, tile_step, n_tiles, transposed):
    tok = [pl.BlockSpec((TM, D_MODEL),
                        lambda l, t, own=own: (jnp.where(l == own, first_tile + t * tile_step, first_tile), 0))
           for own in range(DEPTH)]
    if transposed:
        out = pl.BlockSpec((None, None, B_HEADS, B_HD, TM), lambda l, t: (l, t, 0, 0, 0))
        shape = jax.ShapeDtypeStruct((DEPTH, n_tiles, B_HEADS, B_HD, TM), F32)
    else:
        out = pl.BlockSpec((None, TM, B_W), lambda l, t: (l, t, 0))
        shape = jax.ShapeDtypeStruct((DEPTH, n_tiles * TM, B_W), F32)
    return pl.pallas_call(
        functools.partial(_kv_tails_kernel, transposed=transposed),
        grid=(DEPTH, n_tiles),
        in_specs=tok + [pl.BlockSpec((None, 1, D_MODEL), lambda l, t: (l, 0, 0)),
                        pl.BlockSpec((None,) + w_kv.shape[1:], lambda l, t: (l, 0, 0))],
        out_specs=[out, out],
        out_shape=[shape, shape],
        compiler_params=pltpu.CompilerParams(dimension_semantics=("arbitrary", "arbitrary"),
                                             vmem_limit_bytes=VMEM_LIMIT),
        name="kv_tails_prompt" if transposed else "kv_tails_sample",
    )(*xs, norms, w_kv)


def _gla_kernel(q_ref, k_ref, v_ref, la_ref, r_ref, s0_ref, gn_ref, g_ref, sout_ref,
                qbd_ref, kdec_ref, inc_ref, dec_ref, s_ref, *, chunk, group, n_sub):
    c = chunk
    rows = group * c
    r_i = lax.broadcasted_iota(jnp.int32, (rows, rows), 0)
    c_i = lax.broadcasted_iota(jnp.int32, (rows, rows), 1)
    same_chunk = (r_i & -c) == (c_i & -c)
    tri = jnp.where(same_chunk & (r_i >= c_i), 1.0, 0.0).astype(BF)
    causal = lax.broadcasted_iota(jnp.int32, (c, c), 0) >= lax.broadcasted_iota(jnp.int32, (c, c), 1)
    lane = lax.broadcasted_iota(jnp.int32, (c, A_QK), 1)
    head_mask = [(lane >= h * A_DK) & (lane < (h + 1) * A_DK) for h in range(A_HEADS)]
    pair = rows >= LANES
    if pair:
        pos = lax.broadcasted_iota(jnp.int32, (A_DK, LANES), 1)
        pos_mask = [pos < c, pos >= c]
    tail = 8 if group * 8 >= 32 else c
    gn = gn_ref[...]

    def sub_rows(t):
        start = t * rows
        return start if isinstance(start, int) else pl.multiple_of(start, rows)

    def step(te, se, tp, sp):
        if tp is not None:
            rp = pl.ds(sub_rows(tp), rows)
            la = la_ref[rp, :]
            la_hi = la.astype(BF)
            rem = la - la_hi.astype(F32)
            la_mid = rem.astype(BF)
            la_lo = (rem - la_mid.astype(F32)).astype(BF)
            b3 = _dot(tri, jnp.concatenate([la_hi, la_mid, la_lo], axis=1))
        if te is not None:
            row0 = sub_rows(te)
            re = pl.ds(row0, rows)
            states = [s_ref[...]]
            for ci in range(group):
                states.append(dec_ref[se, ci] * states[-1] + inc_ref[se, ci])
            s_ref[...] = states[-1]
            q_bd = [qbd_ref[se, ci] for ci in range(group)]
            scores = [_dot_nt(q_bd[ci], kdec_ref[se, ci * c:(ci + 1) * c, :]) for ci in range(group)]
            o_inter = [_dot(q_bd[ci], states[ci].astype(BF)) for ci in range(group)]
        if tp is not None:
            b = b3[:, :A_QK] + b3[:, A_QK:2 * A_QK] + b3[:, 2 * A_QK:]
            k = k_ref[rp, :]
            q_dec = (q_ref[rp, :] * (A_DK ** -0.5) * jnp.exp(b)).astype(BF)
            kdec_ref[sp] = (k * jnp.exp(-b)).astype(BF)
            b_last = jnp.concatenate(
                [jnp.broadcast_to(b[(ci + 1) * c - 1:(ci + 1) * c, :], (c, A_QK)) for ci in range(group)], axis=0)
            ku_t = (k * jnp.exp(b_last - b)).T.astype(BF)
            tail_t = jnp.concatenate([b[(ci + 1) * c - tail:(ci + 1) * c, :] for ci in range(group)], axis=0).T
            for ci in range(group):
                q_c = q_dec[ci * c:(ci + 1) * c, :]
                qbd_ref[sp, ci] = jnp.concatenate(
                    [jnp.where(head_mask[h], q_c, jnp.zeros_like(q_c)) for h in range(A_HEADS)], axis=0)
                decay = jnp.exp(tail_t[:, (ci + 1) * tail - 1:(ci + 1) * tail])
                dec_ref[sp, ci] = jnp.broadcast_to(decay, (A_QK, A_DV))
        if te is not None:
            v = v_ref[re, :]
            r = r_ref[re, :]
            for ci in range(group):
                cr = slice(ci * c, (ci + 1) * c)
                outs = []
                for h in range(A_HEADS):
                    hs = slice(h * c, (h + 1) * c)
                    hv = slice(h * A_DV, (h + 1) * A_DV)
                    s_h = jnp.where(causal, scores[ci][hs, :], 0.0).astype(BF)
                    o = _dot(s_h, v[cr, hv]) + o_inter[ci][hs, :]
                    outs.append((_rmsnorm(o, gn) * _silu(r[cr, hv])).astype(BF))
                g_ref[pl.ds(row0 + ci * c, c), :] = jnp.concatenate(outs, axis=1)
        if tp is not None:
            v = v_ref[rp, :]
            for ci in range(group):
                incs = []
                for h in range(A_HEADS):
                    hk = slice(h * A_DK, (h + 1) * A_DK)
                    hv = slice(h * A_DV, (h + 1) * A_DV)
                    if pair:
                        p0 = (ci // 2) * LANES
                        lhs = ku_t[hk, p0:p0 + LANES]
                        lhs = jnp.where(pos_mask[ci % 2], lhs, jnp.zeros_like(lhs))
                        incs.append(_dot(lhs, v[p0:p0 + LANES, hv]))
                    else:
                        incs.append(_dot(ku_t[hk, :], v[:, hv]))
                inc_ref[sp, ci] = jnp.concatenate(incs, axis=0)

    s_ref[...] = s0_ref[...]
    step(None, None, 0, 0)
    if n_sub > 1:
        def body(j, carry):
            t = 2 * j
            step(t, 0, t + 1, 1)
            step(t + 1, 1, t + 2, 0)
            return carry

        lax.fori_loop(0, n_sub // 2 - 1, body, 0)
        step(n_sub - 2, 0, n_sub - 1, 1)
        step(n_sub - 1, 1, None, None)
    else:
        step(0, 0, None, None)
    sout_ref[...] = s_ref[...]


def _gla_call(aq, ak, av, la, ar, s0, gn, *, row0, seq, chunk, group, name):
    n_streams = s0.shape[0]
    rows = chunk * group
    n_sub = seq // rows
    assert seq % rows == 0 and (n_sub == 1 or n_sub % 2 == 0) and chunk & (chunk - 1) == 0
    base = row0 // seq
    tok = lambda w: pl.BlockSpec((seq, w), lambda b: (base + b, 0))
    st = pl.BlockSpec((None, A_QK, A_DV), lambda b: (b, 0, 0))
    return pl.pallas_call(
        functools.partial(_gla_kernel, chunk=chunk, group=group, n_sub=n_sub),
        grid=(n_streams,),
        in_specs=[tok(A_QK), tok(A_QK), tok(A_VW), tok(A_QK), tok(A_VW), st,
                  pl.BlockSpec((1, A_DV), lambda b: (0, 0))],
        out_specs=[pl.BlockSpec((seq, A_VW), lambda b: (b, 0)), st],
        out_shape=[jax.ShapeDtypeStruct((n_streams * seq, A_VW), BF),
                   jax.ShapeDtypeStruct((n_streams, A_QK, A_DV), F32)],
        scratch_shapes=[pltpu.VMEM((2, group, A_HEADS * chunk, A_QK), BF), pltpu.VMEM((2, rows, A_QK), BF),
                        pltpu.VMEM((2, group, A_QK, A_DV), F32), pltpu.VMEM((2, group, A_QK, A_DV), F32),
                        pltpu.VMEM((A_QK, A_DV), F32)],
        compiler_params=pltpu.CompilerParams(dimension_semantics=("arbitrary",), vmem_limit_bytes=VMEM_LIMIT),
        name=name,
    )(aq, ak, av, la, ar, s0, gn)


def _bias_block(brow_ref, blk, tq, tk):
    rows = []
    for h in range(4):
        f = jnp.broadcast_to(brow_ref[blk * 4 + h:blk * 4 + h + 1, :], (tq, BIAS_ROW))
        rows.append(pltpu.roll(f, BIAS_ROW - CHUNK, 1, stride=1, stride_axis=0)[:, :tk])
    return jnp.concatenate(rows, axis=0)


def _block_diag_rows(qb):
    lane = lax.broadcasted_iota(jnp.int32, qb.shape, 1)
    return jnp.concatenate(
        [jnp.where((lane >= h * B_HD) & (lane < (h + 1) * B_HD), qb, jnp.zeros_like(qb)) for h in range(4)],
        axis=0)


def _attn_scores(qb, kw, valid):
    s = _dot_nt(_block_diag_rows(qb), kw)
    return s if valid is None else jnp.where(valid, s, -1e30)


def _attn_softmax(s, bias_tail):
    s = jnp.concatenate([s[:, :BIAS_FLAT], s[:, BIAS_FLAT:] + bias_tail], axis=1)
    p = jnp.exp2(s - jnp.max(s, axis=-1, keepdims=True))
    l = jnp.sum(p, axis=-1, keepdims=True)
    return p.astype(BF), jnp.broadcast_to(l, (s.shape[0], LANES))


def _attn_norm(o, l, gain, diag, tq):
    o = jnp.where(diag, o, 0.0)
    inv = 1.0 / l
    ms = jnp.sum(o * o, axis=-1, keepdims=True) * (inv * inv * (1.0 / B_HD))
    f = inv * lax.rsqrt(ms + EPS)
    y = o * jnp.concatenate([f, f], axis=1)
    return (y[0:tq] + y[tq:2 * tq] + y[2 * tq:3 * tq] + y[3 * tq:4 * tq]) * gain


def _diag_mask(tq):
    r = lax.broadcasted_iota(jnp.int32, (4 * tq, 4 * B_HD), 0)
    cidx = lax.broadcasted_iota(jnp.int32, (4 * tq, 4 * B_HD), 1)
    m = None
    for h in range(4):
        t = (r >= h * tq) & (r < (h + 1) * tq) & (cidx >= h * B_HD) & (cidx < (h + 1) * B_HD)
        m = t if m is None else (m | t)
    return m


def _attn_prompt_kernel(q_ref, k_ref, v_ref, brow_ref, gain_ref, o_ref,
                        kpad_ref, vpad_ref, bias_ref, s_ref, p_ref, l_ref, acc_ref, *, seq):
    tk = B_LEFT + CHUNK
    n_chunks = seq // CHUNK
    kpad_ref[0:B_LEFT, :] = jnp.zeros((B_LEFT, B_W), BF)
    vpad_ref[0:B_LEFT, :] = jnp.zeros((B_LEFT, B_W), BF)
    kpad_ref[B_LEFT:, :] = k_ref[...]
    vpad_ref[B_LEFT:, :] = v_ref[...]
    for blk in range(2):
        bias_ref[blk] = _bias_block(brow_ref, blk, CHUNK, tk)[:, BIAS_FLAT:]
    gain = gain_ref[...]
    diag = _diag_mask(CHUNK)
    key_pos = lax.broadcasted_iota(jnp.int32, (1, tk), 1)
    blocks = [slice(blk * 4 * B_HD, (blk + 1) * 4 * B_HD) for blk in range(2)]

    def scores_stage(ci, slot, masked, blk):
        r0 = pl.multiple_of(ci * CHUNK, CHUNK)
        valid = (key_pos + r0 >= B_LEFT) if masked else None
        cs = blocks[blk]
        s_ref[slot, blk] = _attn_scores(q_ref[pl.ds(r0, CHUNK), cs], kpad_ref[pl.ds(r0, tk), cs], valid)

    def softmax_stage(slot, blk):
        p, l = _attn_softmax(s_ref[slot, blk], bias_ref[blk])
        p_ref[slot, blk] = p
        l_ref[slot, blk] = l

    def values_stage(ci, slot, blk):
        r0 = pl.multiple_of(ci * CHUNK, CHUNK)
        acc_ref[slot, blk] = _dot(p_ref[slot, blk], vpad_ref[pl.ds(r0, tk), blocks[blk]])

    def norm_stage(ci, slot, blk):
        r0 = pl.multiple_of(ci * CHUNK, CHUNK)
        o_ref[pl.ds(r0, CHUNK), blocks[blk]] = _attn_norm(
            acc_ref[slot, blk], l_ref[slot, blk], gain, diag, CHUNK).astype(BF)

    def step(i, parity, masked, live):
        for blk in range(2):
            if live[0]:
                scores_stage(i, parity, masked, blk)
            if live[3]:
                norm_stage(i - 3, 1 - parity, blk)
            if live[1]:
                softmax_stage(1 - parity, blk)
            if live[2]:
                values_stage(i - 2, parity, blk)

    def static_step(i, masked):
        step(i, i % 2, masked, [0 <= i - k < n_chunks for k in range(4)])

    def pair(masked):
        def body(j, carry):
            for d in range(2):
                step(2 * j + d, d, masked, [True] * 4)
            return carry
        return body

    n_masked = B_LEFT // CHUNK
    for i in range(4):
        static_step(i, True)
    lax.fori_loop(2, n_masked // 2, pair(True), 0)
    lax.fori_loop(n_masked // 2, n_chunks // 2, pair(False), 0)
    for i in range(n_chunks, n_chunks + 3):
        static_step(i, False)


def _attn_prompt_call(bq, bk, bv, brow, gain, *, n_streams, seq):
    tok = pl.BlockSpec((seq, B_W), lambda b: (b, 0))
    return pl.pallas_call(
        functools.partial(_attn_prompt_kernel, seq=seq),
        grid=(n_streams,),
        in_specs=[tok, tok, tok, pl.BlockSpec(brow.shape, lambda b: (0, 0)),
                  pl.BlockSpec(gain.shape, lambda b: (0, 0))],
        out_specs=tok,
        out_shape=jax.ShapeDtypeStruct((n_streams * seq, B_W), BF),
        scratch_shapes=[pltpu.VMEM((B_LEFT + seq, B_W), BF), pltpu.VMEM((B_LEFT + seq, B_W), BF),
                        pltpu.VMEM((2, 4 * CHUNK, B_LEFT + CHUNK - BIAS_FLAT), F32),
                        pltpu.VMEM((2, 2, 4 * CHUNK, B_LEFT + CHUNK), F32),
                        pltpu.VMEM((2, 2, 4 * CHUNK, B_LEFT + CHUNK), BF),
                        pltpu.VMEM((2, 2, 4 * CHUNK, LANES), F32),
                        pltpu.VMEM((2, 2, 4 * CHUNK, 4 * B_HD), F32)],
        compiler_params=pltpu.CompilerParams(dimension_semantics=("arbitrary",), vmem_limit_bytes=VMEM_LIMIT),
        name="attn_prompt",
    )(bq, bk, bv, brow, gain)


def _attn_sample_kernel(q_ref, k_ref, v_ref, ck_ref, cv_ref, brow_ref, gain_ref, o_ref, *, tq):
    w = ck_ref.shape[-1]
    q = q_ref[...].astype(F32)
    k_new = k_ref[...].astype(F32)
    v_new = v_ref[...].astype(F32)
    gain = gain_ref[:, :B_HD]
    outs = []
    for h in range(B_HEADS):
        cols = slice(h * B_HD, (h + 1) * B_HD)
        q_h = q[:, cols].astype(BF)
        f = jnp.broadcast_to(brow_ref[h:h + 1, :], (tq, BIAS_ROW))
        bias = pltpu.roll(f, BIAS_ROW - CHUNK, 1, stride=1, stride_axis=0)[:, :w + tq]
        s = jnp.concatenate([_dot(q_h, ck_ref[h].astype(BF)), _dot_nt(q_h, k_new[:, cols].astype(BF))], axis=1) + bias
        p = jnp.exp2(s - jnp.max(s, axis=-1, keepdims=True))
        l = jnp.sum(p, axis=-1, keepdims=True)
        p = p.astype(BF)
        o = (_dot_nt(p[:, :w], cv_ref[h].astype(BF)) + _dot(p[:, w:], v_new[:, cols].astype(BF))) / l
        outs.append(_rmsnorm(o, gain))
    o_ref[...] = jnp.concatenate(outs, axis=1).astype(BF)


def _attn_sample_call(bq, bk, bv, cache_k, cache_v, brow, gain, *, layer, row0, tq):
    n_streams, w = cache_k.shape[1], cache_k.shape[-1]
    base = row0 // tq
    tok = pl.BlockSpec((tq, B_W), lambda b: (base + b, 0))
    cache = pl.BlockSpec((None, None, B_HEADS, B_HD, w), lambda b: (layer, b, 0, 0, 0))
    return pl.pallas_call(
        functools.partial(_attn_sample_kernel, tq=tq),
        grid=(n_streams,),
        in_specs=[tok, tok, tok, cache, cache, pl.BlockSpec(brow.shape, lambda b: (0, 0)),
                  pl.BlockSpec(gain.shape, lambda b: (0, 0))],
        out_specs=pl.BlockSpec((tq, B_W), lambda b: (b, 0)),
        out_shape=jax.ShapeDtypeStruct((n_streams * tq, B_W), BF),
        compiler_params=pltpu.CompilerParams(dimension_semantics=("arbitrary",), vmem_limit_bytes=VMEM_LIMIT),
        name="attn_sample",
    )(bq, bk, bv, cache_k, cache_v, brow, gain)


def kernel(x_prompt, x_sample, state_gla, cache_k, cache_v, norm_ffn1, w_ffn1_gate, w_ffn1_up, w_ffn1_down,
           norm_mix, w_in, w_alpha, b_alpha, gla_norm, attn_bias, attn_norm, w_out, norm_ffn2, w_ffn2_gate,
           w_ffn2_up, w_ffn2_down, final_norm):
    batch, seq, _ = x_prompt.shape
    dec_batch, dec_seq, _ = x_sample.shape
    n_prompt = batch * seq
    n_sample = dec_batch * dec_seq
    past = cache_k.shape[2]
    assert seq % TM == 0 and n_sample % TM == 0 and TM == B_LEFT and past == B_LEFT
    assert dec_seq <= CHUNK
    assert seq % (2 * CHUNK) == 0 and seq >= B_LEFT + 2 * CHUNK

    x = (x_prompt.reshape(n_prompt, D_MODEL), x_sample.reshape(n_sample, D_MODEL))
    row = lambda v: v.reshape(1, -1).astype(F32)
    sizes = dict(n_prompt=n_prompt, n_sample=n_sample)
    cache_t = lambda c: jnp.transpose(c, (0, 1, 3, 4, 2))
    ck_t, cv_t = cache_t(cache_k), cache_t(cache_v)

    o_gate = 2 * A_QK + A_VW
    o_r = o_gate + A_RANK
    o_b = o_r + A_VW

    gla_p, gla_s, proj_in = [], [], []
    zero_state = jnp.zeros((batch, A_QK, A_DV), F32)
    bf = lambda w: w.astype(BF)
    ffn1 = (bf(w_ffn1_gate), bf(w_ffn1_up), bf(w_ffn1_down))
    ffn2 = (bf(w_ffn2_gate), bf(w_ffn2_up), bf(w_ffn2_down))
    wo = bf(w_out)
    w_proj = (bf(w_in[:, :, :o_gate]), bf(jnp.pad(w_in[:, :, o_gate:o_r], ((0, 0), (0, 0), (0, LANES - A_RANK)))),
              bf(w_in[:, :, o_r:o_b]), bf(w_in[:, :, o_b:]),
              bf(jnp.pad(w_alpha, ((0, 0), (0, LANES - A_RANK), (0, 0)))))
    for l in range(DEPTH):
        x = _ffn_call(x, row(norm_ffn1[l]), *ffn1, layer=l, **sizes)

        proj_in.append(x)
        aq, ak, av, la, ar, bq, bk, bv = _inproj_call(x, row(norm_mix[l]), *w_proj, row(b_alpha[l]), layer=l)

        gn = row(gla_norm[l])
        g_p, sp = _gla_call(aq, ak, av, la, ar, zero_state, gn, row0=0, seq=seq, chunk=CHUNK,
                            group=GLA_GROUP, name="gla_prompt")
        g_s, ss = _gla_call(aq, ak, av, la, ar, state_gla[l].reshape(dec_batch, A_QK, A_DV), gn,
                            row0=n_prompt, seq=dec_seq, chunk=dec_seq, group=1, name="gla_sample")

        brow = jnp.pad(attn_bias[l][:, ::-1],
                       ((0, 0), (B_LEFT - CHUNK, BIAS_ROW - (B_LEFT - CHUNK) - (2 * REL_CLIP + 1))), mode="edge")
        brow = (brow - brow[:, :1]) * LOG2E
        gain = jnp.tile(row(attn_norm[l]), (1, 4))
        a_p = _attn_prompt_call(bq, bk, bv, brow, gain, n_streams=batch, seq=seq)
        a_s = _attn_sample_call(bq, bk, bv, ck_t, cv_t, brow, gain,
                                layer=l, row0=n_prompt, tq=dec_seq)

        x = _ffn_call(x, row(norm_ffn2[l]), *ffn2, layer=l, mix=((g_p, g_s), (a_p, a_s), wo),
                      final_norm=row(final_norm) if l == DEPTH - 1 else None, **sizes)
        gla_p.append(sp.reshape(batch, A_HEADS, A_DK, A_DV))
        gla_s.append(ss.reshape(dec_batch, A_HEADS, A_DK, A_DV))

    norms = norm_mix.reshape(DEPTH, 1, D_MODEL).astype(F32)
    w_kv = w_in[:, :, o_b + B_W:]
    tiles_per_stream = seq // TM
    k_p, v_p = _kv_tails_call(proj_in, norms, jnp.swapaxes(w_kv, 1, 2).astype(BF), first_tile=tiles_per_stream - 1,
                              tile_step=tiles_per_stream, n_tiles=batch, transposed=True)
    k_s, v_s = _kv_tails_call(proj_in, norms, w_kv.astype(BF), first_tile=n_prompt // TM, tile_step=1,
                              n_tiles=n_sample // TM, transposed=False)
    tail_rows = lambda t: jnp.transpose(t, (0, 1, 4, 2, 3))
    y_prompt, y_sample = x
    return (y_prompt.reshape(batch, seq, D_MODEL), y_sample.reshape(dec_batch, dec_seq, D_MODEL),
            jnp.stack(gla_p), tail_rows(k_p), tail_rows(v_p), jnp.stack(gla_s),
            k_s.reshape(DEPTH, dec_batch, dec_seq, B_HEADS, B_HD),
            v_s.reshape(DEPTH, dec_batch, dec_seq, B_HEADS, B_HD))
```

```python
import functools

import jax
import jax.numpy as jnp
from jax import lax
from jax.experimental import pallas as pl
from jax.experimental.pallas import tpu as pltpu

F32 = jnp.float32
BF = jnp.bfloat16

D_MODEL = 1024
D_FF = 2816
DEPTH = 2
CHUNK = 64
A_HEADS = 4
A_DK = 64
A_DV = 128
A_QK = A_HEADS * A_DK
A_VW = A_HEADS * A_DV
A_RANK = 16
A_TAU = 16.0
B_HEADS = 8
B_HD = 64
B_W = B_HEADS * B_HD
B_LEFT = 8 * CHUNK
REL_CLIP = 128
EPS = 1e-6

TM = 512
FF_CHUNK = 256
GLA_GROUP = 4
LANES = 128
BIAS_ROW = 768
BIAS_FLAT = B_LEFT - REL_CLIP
VMEM_LIMIT = 56 * 1024 * 1024
LOG2E = 1.4426950408889634
Q_SCALE = B_HD ** -0.5 * LOG2E


def _rmsnorm(x, g):
    ms = jnp.mean(x * x, axis=-1, keepdims=True)
    return x * lax.rsqrt(ms + EPS) * g


def _rms_parts(x, g):
    scale = lax.rsqrt(jnp.mean(x * x, axis=-1, keepdims=True) + EPS)
    return (x * g).astype(BF), scale


def _silu(x):
    return x * jax.nn.sigmoid(x)


def _dot(a, b):
    return jnp.dot(a, b, preferred_element_type=F32)


def _dot_nt(a, b):
    return lax.dot_general(a, b, (((1,), (1,)), ((), ())), preferred_element_type=F32)


def _const_spec(shape):
    return pl.BlockSpec(shape, lambda *_: (0,) * len(shape), pipeline_mode=pl.Buffered(1))


def _ffn_kernel(*refs, first, mix, final, prompt_tiles):
    refs = list(refs)
    i = pl.program_id(0)
    if first:
        xp_ref, xs_ref = refs[:2]
        refs = refs[2:]
        x = jnp.where(i < prompt_tiles, xp_ref[...], xs_ref[...])
    else:
        x = refs.pop(0)[...]
    if mix:
        gp_ref, gs_ref, ap_ref, as_ref, wog_ref, woa_ref = refs[:6]
        refs = refs[6:]
        g = jnp.where(i < prompt_tiles, gp_ref[...], gs_ref[...])
        a = jnp.where(i < prompt_tiles, ap_ref[...], as_ref[...])
        x = x + _dot(g, wog_ref[...]) + _dot(a, woa_ref[...])
    n_ref, wg_ref, wu_ref, wd_ref = refs[:4]
    refs = refs[4:]
    if final:
        fn_ref = refs.pop(0)
    act_ref = refs.pop()

    h, scale = _rms_parts(x, n_ref[...])
    for c in range(D_FF // FF_CHUNK):
        sl = slice(c * FF_CHUNK, (c + 1) * FF_CHUNK)
        g = _dot(h, wg_ref[:, sl]) * scale
        u = _dot(h, wu_ref[:, sl]) * scale
        act_ref[:, sl] = (_silu(g) * u).astype(BF)
    y = x + 0.5 * _dot(act_ref[...], wd_ref[...])
    if not final:
        refs[0][...] = y
        return
    y = _rmsnorm(y, fn_ref[...])
    yp_ref, ys_ref = refs

    @pl.when(i < prompt_tiles)
    def _():
        yp_ref[...] = y

    @pl.when(i >= prompt_tiles)
    def _():
        ys_ref[...] = y


def _split_specs(width, prompt_tiles):
    return [pl.BlockSpec((TM, width), lambda i: (jnp.minimum(i, prompt_tiles - 1), 0)),
            pl.BlockSpec((TM, width), lambda i: (jnp.maximum(i - prompt_tiles, 0), 0))]


def _layer_spec(shape, layer, block=0):
    return pl.BlockSpec((None,) + shape, lambda *_: (layer, block) + (0,) * (len(shape) - 1),
                        pipeline_mode=pl.Buffered(1))


def _ffn_call(x, norm, wg, wu, wd, *, layer, n_prompt, n_sample, mix=None, final_norm=None):
    n = n_prompt + n_sample
    prompt_tiles = n_prompt // TM
    first = isinstance(x, tuple)
    tok = lambda w: pl.BlockSpec((TM, w), lambda i: (i, 0))
    args = list(x) if first else [x]
    specs = _split_specs(D_MODEL, prompt_tiles) if first else [tok(D_MODEL)]
    if mix is not None:
        g, a, wo = mix
        args += [*g, *a, wo, wo]
        specs += _split_specs(A_VW, prompt_tiles) + _split_specs(B_W, prompt_tiles)
        specs += [_layer_spec((A_VW, D_MODEL), layer, 0), _layer_spec((B_W, D_MODEL), layer, A_VW // B_W)]
    args += [norm, wg, wu, wd]
    specs += [_const_spec((1, D_MODEL)), _layer_spec((D_MODEL, D_FF), layer), _layer_spec((D_MODEL, D_FF), layer),
              _layer_spec((D_FF, D_MODEL), layer)]
    if final_norm is not None:
        args.append(final_norm)
        specs.append(_const_spec((1, D_MODEL)))
        out_specs = _split_specs(D_MODEL, prompt_tiles)
        out_shape = [jax.ShapeDtypeStruct((n_prompt, D_MODEL), F32), jax.ShapeDtypeStruct((n_sample, D_MODEL), F32)]
    else:
        out_specs = tok(D_MODEL)
        out_shape = jax.ShapeDtypeStruct((n, D_MODEL), F32)
    return pl.pallas_call(
        functools.partial(_ffn_kernel, first=first, mix=mix is not None, final=final_norm is not None,
                          prompt_tiles=prompt_tiles),
        grid=(n // TM,),
        in_specs=specs,
        out_specs=out_specs,
        out_shape=out_shape,
        scratch_shapes=[pltpu.VMEM((TM, D_FF), BF)],
        compiler_params=pltpu.CompilerParams(dimension_semantics=("arbitrary",), vmem_limit_bytes=VMEM_LIMIT),
        name="ffn_mix" if mix is not None else "ffn",
    )(*args)


def _inproj_kernel(x_ref, n_ref, wa_ref, wgate_ref, wr_ref, wb_ref, walpha_ref, balpha_ref,
                   aq_ref, ak_ref, av_ref, la_ref, ar_ref, bq_ref, bk_ref, bv_ref):
    h, scale = _rms_parts(x_ref[...], n_ref[...])
    ag = _dot(h, wgate_ref[...]) * scale
    za = _dot(h, wa_ref[...]) * scale
    pre = _dot(ag.astype(BF), walpha_ref[...]) + balpha_ref[...]
    aq_ref[...] = za[:, :A_QK]
    ak_ref[...] = za[:, A_QK:2 * A_QK]
    av_ref[...] = za[:, 2 * A_QK:].astype(BF)
    ar_ref[...] = _dot(h, wr_ref[...]) * scale
    log_sig = jnp.minimum(pre, 0.0) - jnp.log1p(jnp.exp(-jnp.abs(pre)))
    la_ref[...] = log_sig * (1.0 / A_TAU)
    zb = _dot(h, wb_ref[...]) * scale
    bq_ref[...] = (zb[:, :B_W] * Q_SCALE).astype(BF)
    bk_ref[...] = zb[:, B_W:2 * B_W].astype(BF)
    bv_ref[...] = zb[:, 2 * B_W:].astype(BF)


def _inproj_call(x, norm, wa, wgate, wr, wb, walpha, balpha, *, layer):
    n = x.shape[0]
    tok = lambda w: pl.BlockSpec((TM, w), lambda i: (i, 0))
    outs = [
        (A_QK, F32), (A_QK, F32), (A_VW, BF), (A_QK, F32), (A_VW, F32), (B_W, BF), (B_W, BF), (B_W, BF)]
    return pl.pallas_call(
        _inproj_kernel,
        grid=(n // TM,),
        in_specs=[tok(D_MODEL), _const_spec((1, D_MODEL))]
        + [_layer_spec(w.shape[1:], layer) for w in (wa, wgate, wr, wb, walpha)] + [_const_spec(balpha.shape)],
        out_specs=[tok(w) for w, _ in outs],
        out_shape=[jax.ShapeDtypeStruct((n, w), dt) for w, dt in outs],
        compiler_params=pltpu.CompilerParams(dimension_semantics=("arbitrary",), vmem_limit_bytes=VMEM_LIMIT),
        name="inproj",
    )(x, norm, wa, wgate, wr, wb, walpha, balpha)


def _kv_tails_kernel(*refs, transposed):
    x_refs = refs[:DEPTH]
    n_ref, w_ref, k_ref, v_ref = refs[DEPTH:]
    layer = pl.program_id(0)
    x = x_refs[0][...]
    for l in range(1, DEPTH):
        x = jnp.where(layer == l, x_refs[l][...], x)
    h = _rmsnorm(x, n_ref[...]).astype(BF)
    if transposed:
        kv = _dot_nt(w_ref[...], h)
        k_ref[...] = kv[:B_W].reshape(B_HEADS, B_HD, TM)
        v_ref[...] = kv[B_W:].reshape(B_HEADS, B_HD, TM)
    else:
        kv = _dot(h, w_ref[...])
        k_ref[...] = kv[:, :B_W]
        v_ref[...] = kv[:, B_W:]


def _kv_tails_call(xs, norms, w_kv, *, first_tile, tile_step, n_tiles, transposed):
    tok = [pl.BlockSpec((TM, D_MODEL),
                        lambda l, t, own=own: (jnp.where(l == own, first_tile + t * tile_step, first_tile), 0))
           for own in range(DEPTH)]
    if transposed:
        out = pl.BlockSpec((None, None, B_HEADS, B_HD, TM), lambda l, t: (l, t, 0, 0, 0))
        shape = jax.ShapeDtypeStruct((DEPTH, n_tiles, B_HEADS, B_HD, TM), F32)
    else:
        out = pl.BlockSpec((None, TM, B_W), lambda l, t: (l, t, 0))
        shape = jax.ShapeDtypeStruct((DEPTH, n_tiles * TM, B_W), F32)
    return pl.pallas_call(
        functools.partial(_kv_tails_kernel, transposed=transposed),
        grid=(DEPTH, n_tiles),
        in_specs=tok + [pl.BlockSpec((None, 1, D_MODEL), lambda l, t: (l, 0, 0)),
                        pl.BlockSpec((None,) + w_kv.shape[1:], lambda l, t: (l, 0, 0))],
        out_specs=[out, out],
        out_shape=[shape, shape],
        compiler_params=pltpu.CompilerParams(dimension_semantics=("arbitrary", "arbitrary"),
                                             vmem_limit_bytes=VMEM_LIMIT),
        name="kv_tails_prompt" if transposed else "kv_tails_sample",
    )(*xs, norms, w_kv)


def _gla_kernel(q_ref, k_ref, v_ref, la_ref, r_ref, s0_ref, gn_ref, g_ref, sout_ref,
                qbd_ref, kdec_ref, inc_ref, dec_ref, s_ref, *, chunk, group, n_sub):
    c = chunk
    rows = group * c
    r_i = lax.broadcasted_iota(jnp.int32, (rows, rows), 0)
    c_i = lax.broadcasted_iota(jnp.int32, (rows, rows), 1)
    same_chunk = (r_i & -c) == (c_i & -c)
    tri = jnp.where(same_chunk & (r_i >= c_i), 1.0, 0.0).astype(BF)
    causal = lax.broadcasted_iota(jnp.int32, (c, c), 0) >= lax.broadcasted_iota(jnp.int32, (c, c), 1)
    lane = lax.broadcasted_iota(jnp.int32, (c, A_QK), 1)
    head_mask = [(lane >= h * A_DK) & (lane < (h + 1) * A_DK) for h in range(A_HEADS)]
    pair = rows >= LANES
    if pair:
        pos = lax.broadcasted_iota(jnp.int32, (A_DK, LANES), 1)
        pos_mask = [pos < c, pos >= c]
    tail = 8 if group * 8 >= 32 else c
    gn = gn_ref[...]

    def sub_rows(t):
        start = t * rows
        return start if isinstance(start, int) else pl.multiple_of(start, rows)

    def step(te, se, tp, sp):
        if tp is not None:
            rp = pl.ds(sub_rows(tp), rows)
            la = la_ref[rp, :]
            la_hi = la.astype(BF)
            rem = la - la_hi.astype(F32)
            la_mid = rem.astype(BF)
            la_lo = (rem - la_mid.astype(F32)).astype(BF)
            b3 = _dot(tri, jnp.concatenate([la_hi, la_mid, la_lo], axis=1))
        if te is not None:
            row0 = sub_rows(te)
            re = pl.ds(row0, rows)
            states = [s_ref[...]]
            for ci in range(group):
                states.append(dec_ref[se, ci] * states[-1] + inc_ref[se, ci])
            s_ref[...] = states[-1]
            q_bd = [qbd_ref[se, ci] for ci in range(group)]
            scores = [_dot_nt(q_bd[ci], kdec_ref[se, ci * c:(ci + 1) * c, :]) for ci in range(group)]
            o_inter = [_dot(q_bd[ci], states[ci].astype(BF)) for ci in range(group)]
        if tp is not None:
            b = b3[:, :A_QK] + b3[:, A_QK:2 * A_QK] + b3[:, 2 * A_QK:]
            k = k_ref[rp, :]
            q_dec = (q_ref[rp, :] * (A_DK ** -0.5) * jnp.exp(b)).astype(BF)
            kdec_ref[sp] = (k * jnp.exp(-b)).astype(BF)
            b_last = jnp.concatenate(
                [jnp.broadcast_to(b[(ci + 1) * c - 1:(ci + 1) * c, :], (c, A_QK)) for ci in range(group)], axis=0)
            ku_t = (k * jnp.exp(b_last - b)).T.astype(BF)
            tail_t = jnp.concatenate([b[(ci + 1) * c - tail:(ci + 1) * c, :] for ci in range(group)], axis=0).T
            for ci in range(group):
                q_c = q_dec[ci * c:(ci + 1) * c, :]
                qbd_ref[sp, ci] = jnp.concatenate(
                    [jnp.where(head_mask[h], q_c, jnp.zeros_like(q_c)) for h in range(A_HEADS)], axis=0)
                decay = jnp.exp(tail_t[:, (ci + 1) * tail - 1:(ci + 1) * tail])
                dec_ref[sp, ci] = jnp.broadcast_to(decay, (A_QK, A_DV))
        if te is not None:
            v = v_ref[re, :]
            r = r_ref[re, :]
            for ci in range(group):
                cr = slice(ci * c, (ci + 1) * c)
                outs = []
                for h in range(A_HEADS):
                    hs = slice(h * c, (h + 1) * c)
                    hv = slice(h * A_DV, (h + 1) * A_DV)
                    s_h = jnp.where(causal, scores[ci][hs, :], 0.0).astype(BF)
                    o = _dot(s_h, v[cr, hv]) + o_inter[ci][hs, :]
                    outs.append((_rmsnorm(o, gn) * _silu(r[cr, hv])).astype(BF))
                g_ref[pl.ds(row0 + ci * c, c), :] = jnp.concatenate(outs, axis=1)
        if tp is not None:
            v = v_ref[rp, :]
            for ci in range(group):
                incs = []
                for h in range(A_HEADS):
                    hk = slice(h * A_DK, (h + 1) * A_DK)
                    hv = slice(h * A_DV, (h + 1) * A_DV)
                    if pair:
                        p0 = (ci // 2) * LANES
                        lhs = ku_t[hk, p0:p0 + LANES]
                        lhs = jnp.where(pos_mask[ci % 2], lhs, jnp.zeros_like(lhs))
                        incs.append(_dot(lhs, v[p0:p0 + LANES, hv]))
                    else:
                        incs.append(_dot(ku_t[hk, :], v[:, hv]))
                inc_ref[sp, ci] = jnp.concatenate(incs, axis=0)

    s_ref[...] = s0_ref[...]
    step(None, None, 0, 0)
    if n_sub > 1:
        def body(j, carry):
            t = 2 * j
            step(t, 0, t + 1, 1)
            step(t + 1, 1, t + 2, 0)
            return carry

        lax.fori_loop(0, n_sub // 2 - 1, body, 0)
        step(n_sub - 2, 0, n_sub - 1, 1)
        step(n_sub - 1, 1, None, None)
    else:
        step(0, 0, None, None)
    sout_ref[...] = s_ref[...]


def _gla_call(aq, ak, av, la, ar, s0, gn, *, row0, seq, chunk, group, name):
    n_streams = s0.shape[0]
    rows = chunk * group
    n_sub = seq // rows
    assert seq % rows == 0 and (n_sub == 1 or n_sub % 2 == 0) and chunk & (chunk - 1) == 0
    base = row0 // seq
    tok = lambda w: pl.BlockSpec((seq, w), lambda b: (base + b, 0))
    st = pl.BlockSpec((None, A_QK, A_DV), lambda b: (b, 0, 0))
    return pl.pallas_call(
        functools.partial(_gla_kernel, chunk=chunk, group=group, n_sub=n_sub),
        grid=(n_streams,),
        in_specs=[tok(A_QK), tok(A_QK), tok(A_VW), tok(A_QK), tok(A_VW), st,
                  pl.BlockSpec((1, A_DV), lambda b: (0, 0))],
        out_specs=[pl.BlockSpec((seq, A_VW), lambda b: (b, 0)), st],
        out_shape=[jax.ShapeDtypeStruct((n_streams * seq, A_VW), BF),
                   jax.ShapeDtypeStruct((n_streams, A_QK, A_DV), F32)],
        scratch_shapes=[pltpu.VMEM((2, group, A_HEADS * chunk, A_QK), BF), pltpu.VMEM((2, rows, A_QK), BF),
                        pltpu.VMEM((2, group, A_QK, A_DV), F32), pltpu.VMEM((2, group, A_QK, A_DV), F32),
                        pltpu.VMEM((A_QK, A_DV), F32)],
        compiler_params=pltpu.CompilerParams(dimension_semantics=("arbitrary",), vmem_limit_bytes=VMEM_LIMIT),
        name=name,
    )(aq, ak, av, la, ar, s0, gn)


def _bias_block(brow_ref, blk, tq, tk):
    rows = []
    for h in range(4):
        f = jnp.broadcast_to(brow_ref[blk * 4 + h:blk * 4 + h + 1, :], (tq, BIAS_ROW))
        rows.append(pltpu.roll(f, BIAS_ROW - CHUNK, 1, stride=1, stride_axis=0)[:, :tk])
    return jnp.concatenate(rows, axis=0)


def _block_diag_rows(qb):
    lane = lax.broadcasted_iota(jnp.int32, qb.shape, 1)
    return jnp.concatenate(
        [jnp.where((lane >= h * B_HD) & (lane < (h + 1) * B_HD), qb, jnp.zeros_like(qb)) for h in range(4)],
        axis=0)


def _attn_scores(qb, kw, valid):
    s = _dot_nt(_block_diag_rows(qb), kw)
    return s if valid is None else jnp.where(valid, s, -1e30)


def _attn_softmax(s, bias_tail):
    s = jnp.concatenate([s[:, :BIAS_FLAT], s[:, BIAS_FLAT:] + bias_tail], axis=1)
    p = jnp.exp2(s - jnp.max(s, axis=-1, keepdims=True))
    l = jnp.sum(p, axis=-1, keepdims=True)
    return p.astype(BF), jnp.broadcast_to(l, (s.shape[0], LANES))


def _attn_norm(o, l, gain, diag, tq):
    o = jnp.where(diag, o, 0.0)
    inv = 1.0 / l
    ms = jnp.sum(o * o, axis=-1, keepdims=True) * (inv * inv * (1.0 / B_HD))
    f = inv * lax.rsqrt(ms + EPS)
    y = o * jnp.concatenate([f, f], axis=1)
    return (y[0:tq] + y[tq:2 * tq] + y[2 * tq:3 * tq] + y[3 * tq:4 * tq]) * gain


def _diag_mask(tq):
    r = lax.broadcasted_iota(jnp.int32, (4 * tq, 4 * B_HD), 0)
    cidx = lax.broadcasted_iota(jnp.int32, (4 * tq, 4 * B_HD), 1)
    m = None
    for h in range(4):
        t = (r >= h * tq) & (r < (h + 1) * tq) & (cidx >= h * B_HD) & (cidx < (h + 1) * B_HD)
        m = t if m is None else (m | t)
    return m


def _attn_prompt_kernel(q_ref, k_ref, v_ref, brow_ref, gain_ref, o_ref,
                        kpad_ref, vpad_ref, bias_ref, s_ref, p_ref, l_ref, acc_ref, *, seq):
    tk = B_LEFT + CHUNK
    n_chunks = seq // CHUNK
    kpad_ref[0:B_LEFT, :] = jnp.zeros((B_LEFT, B_W), BF)
    vpad_ref[0:B_LEFT, :] = jnp.zeros((B_LEFT, B_W), BF)
    kpad_ref[B_LEFT:, :] = k_ref[...]
    vpad_ref[B_LEFT:, :] = v_ref[...]
    for blk in range(2):
        bias_ref[blk] = _bias_block(brow_ref, blk, CHUNK, tk)[:, BIAS_FLAT:]
    gain = gain_ref[...]
    diag = _diag_mask(CHUNK)
    key_pos = lax.broadcasted_iota(jnp.int32, (1, tk), 1)
    blocks = [slice(blk * 4 * B_HD, (blk + 1) * 4 * B_HD) for blk in range(2)]

    def scores_stage(ci, slot, masked, blk):
        r0 = pl.multiple_of(ci * CHUNK, CHUNK)
        valid = (key_pos + r0 >= B_LEFT) if masked else None
        cs = blocks[blk]
        s_ref[slot, blk] = _attn_scores(q_ref[pl.ds(r0, CHUNK), cs], kpad_ref[pl.ds(r0, tk), cs], valid)

    def softmax_stage(slot, blk):
        p, l = _attn_softmax(s_ref[slot, blk], bias_ref[blk])
        p_ref[slot, blk] = p
        l_ref[slot, blk] = l

    def values_stage(ci, slot, blk):
        r0 = pl.multiple_of(ci * CHUNK, CHUNK)
        acc_ref[slot, blk] = _dot(p_ref[slot, blk], vpad_ref[pl.ds(r0, tk), blocks[blk]])

    def norm_stage(ci, slot, blk):
        r0 = pl.multiple_of(ci * CHUNK, CHUNK)
        o_ref[pl.ds(r0, CHUNK), blocks[blk]] = _attn_norm(
            acc_ref[slot, blk], l_ref[slot, blk], gain, diag, CHUNK).astype(BF)

    def step(i, parity, masked, live):
        for blk in range(2):
            if live[0]:
                scores_stage(i, parity, masked, blk)
            if live[3]:
                norm_stage(i - 3, 1 - parity, blk)
            if live[1]:
                softmax_stage(1 - parity, blk)
            if live[2]:
                values_stage(i - 2, parity, blk)

    def static_step(i, masked):
        step(i, i % 2, masked, [0 <= i - k < n_chunks for k in range(4)])

    def pair(masked):
        def body(j, carry):
            for d in range(2):
                step(2 * j + d, d, masked, [True] * 4)
            return carry
        return body

    n_masked = B_LEFT // CHUNK
    for i in range(4):
        static_step(i, True)
    lax.fori_loop(2, n_masked // 2, pair(True), 0)
    lax.fori_loop(n_masked // 2, n_chunks // 2, pair(False), 0)
    for i in range(n_chunks, n_chunks + 3):
        static_step(i, False)


def _attn_prompt_call(bq, bk, bv, brow, gain, *, n_streams, seq):
    tok = pl.BlockSpec((seq, B_W), lambda b: (b, 0))
    return pl.pallas_call(
        functools.partial(_attn_prompt_kernel, seq=seq),
        grid=(n_streams,),
        in_specs=[tok, tok, tok, pl.BlockSpec(brow.shape, lambda b: (0, 0)),
                  pl.BlockSpec(gain.shape, lambda b: (0, 0))],
        out_specs=tok,
        out_shape=jax.ShapeDtypeStruct((n_streams * seq, B_W), BF),
        scratch_shapes=[pltpu.VMEM((B_LEFT + seq, B_W), BF), pltpu.VMEM((B_LEFT + seq, B_W), BF),
                        pltpu.VMEM((2, 4 * CHUNK, B_LEFT + CHUNK - BIAS_FLAT), F32),
                        pltpu.VMEM((2, 2, 4 * CHUNK, B_LEFT + CHUNK), F32),
                        pltpu.VMEM((2, 2, 4 * CHUNK, B_LEFT + CHUNK), BF),
                        pltpu.VMEM((2, 2, 4 * CHUNK, LANES), F32),
                        pltpu.VMEM((2, 2, 4 * CHUNK, 4 * B_HD), F32)],
        compiler_params=pltpu.CompilerParams(dimension_semantics=("arbitrary",), vmem_limit_bytes=VMEM_LIMIT),
        name="attn_prompt",
    )(bq, bk, bv, brow, gain)


def _attn_sample_kernel(q_ref, k_ref, v_ref, ck_ref, cv_ref, brow_ref, gain_ref, o_ref, *, tq):
    w = ck_ref.shape[-1]
    q = q_ref[...].astype(F32)
    k_new = k_ref[...].astype(F32)
    v_new = v_ref[...].astype(F32)
    gain = gain_ref[:, :B_HD]
    heads = [slice(h * B_HD, (h + 1) * B_HD) for h in range(B_HEADS)]
    scores = []
    for h, cols in enumerate(heads):
        q_h = q[:, cols].astype(BF)
        f = jnp.broadcast_to(brow_ref[h:h + 1, :], (tq, BIAS_ROW))
        bias = pltpu.roll(f, BIAS_ROW - CHUNK, 1, stride=1, stride_axis=0)[:, :w + tq]
        scores.append(
            jnp.concatenate([_dot(q_h, ck_ref[h].astype(BF)), _dot_nt(q_h, k_new[:, cols].astype(BF))], axis=1) + bias)
    outs = []
    for h, cols in enumerate(heads):
        s = scores[h]
        p = jnp.exp2(s - jnp.max(s, axis=-1, keepdims=True))
        l = jnp.sum(p, axis=-1, keepdims=True)
        p = p.astype(BF)
        o = (_dot_nt(p[:, :w], cv_ref[h].astype(BF)) + _dot(p[:, w:], v_new[:, cols].astype(BF))) / l
        outs.append(_rmsnorm(o, gain))
    o_ref[...] = jnp.concatenate(outs, axis=1).astype(BF)


def _attn_sample_call(bq, bk, bv, cache_k, cache_v, brow, gain, *, layer, row0, tq):
    n_streams, w = cache_k.shape[1], cache_k.shape[-1]
    base = row0 // tq
    tok = pl.BlockSpec((tq, B_W), lambda b: (base + b, 0))
    cache = pl.BlockSpec((None, None, B_HEADS, B_HD, w), lambda b: (layer, b, 0, 0, 0))
    return pl.pallas_call(
        functools.partial(_attn_sample_kernel, tq=tq),
        grid=(n_streams,),
        in_specs=[tok, tok, tok, cache, cache, pl.BlockSpec(brow.shape, lambda b: (0, 0)),
                  pl.BlockSpec(gain.shape, lambda b: (0, 0))],
        out_specs=pl.BlockSpec((tq, B_W), lambda b: (b, 0)),
        out_shape=jax.ShapeDtypeStruct((n_streams * tq, B_W), BF),
        compiler_params=pltpu.CompilerParams(dimension_semantics=("arbitrary",), vmem_limit_bytes=VMEM_LIMIT),
        name="attn_sample",
    )(bq, bk, bv, cache_k, cache_v, brow, gain)


def kernel(x_prompt, x_sample, state_gla, cache_k, cache_v, norm_ffn1, w_ffn1_gate, w_ffn1_up, w_ffn1_down,
           norm_mix, w_in, w_alpha, b_alpha, gla_norm, attn_bias, attn_norm, w_out, norm_ffn2, w_ffn2_gate,
           w_ffn2_up, w_ffn2_down, final_norm):
    batch, seq, _ = x_prompt.shape
    dec_batch, dec_seq, _ = x_sample.shape
    n_prompt = batch * seq
    n_sample = dec_batch * dec_seq
    past = cache_k.shape[2]
    assert seq % TM == 0 and n_sample % TM == 0 and TM == B_LEFT and past == B_LEFT
    assert dec_seq <= CHUNK
    assert seq % (2 * CHUNK) == 0 and seq >= B_LEFT + 2 * CHUNK

    x = (x_prompt.reshape(n_prompt, D_MODEL), x_sample.reshape(n_sample, D_MODEL))
    row = lambda v: v.reshape(1, -1).astype(F32)
    sizes = dict(n_prompt=n_prompt, n_sample=n_sample)
    cache_t = lambda c: jnp.transpose(c, (0, 1, 3, 4, 2))
    ck_t, cv_t = cache_t(cache_k), cache_t(cache_v)

    o_gate = 2 * A_QK + A_VW
    o_r = o_gate + A_RANK
    o_b = o_r + A_VW

    gla_p, gla_s, proj_in = [], [], []
    zero_state = jnp.zeros((batch, A_QK, A_DV), F32)
    bf = lambda w: w.astype(BF)
    ffn1 = (bf(w_ffn1_gate), bf(w_ffn1_up), bf(w_ffn1_down))
    ffn2 = (bf(w_ffn2_gate), bf(w_ffn2_up), bf(w_ffn2_down))
    wo = bf(w_out)
    w_proj = (bf(w_in[:, :, :o_gate]), bf(jnp.pad(w_in[:, :, o_gate:o_r], ((0, 0), (0, 0), (0, LANES - A_RANK)))),
              bf(w_in[:, :, o_r:o_b]), bf(w_in[:, :, o_b:]),
              bf(jnp.pad(w_alpha, ((0, 0), (0, LANES - A_RANK), (0, 0)))))
    for l in range(DEPTH):
        x = _ffn_call(x, row(norm_ffn1[l]), *ffn1, layer=l, **sizes)

        proj_in.append(x)
        aq, ak, av, la, ar, bq, bk, bv = _inproj_call(x, row(norm_mix[l]), *w_proj, row(b_alpha[l]), layer=l)

        gn = row(gla_norm[l])
        g_p, sp = _gla_call(aq, ak, av, la, ar, zero_state, gn, row0=0, seq=seq, chunk=CHUNK,
                            group=GLA_GROUP, name="gla_prompt")
        g_s, ss = _gla_call(aq, ak, av, la, ar, state_gla[l].reshape(dec_batch, A_QK, A_DV), gn,
                            row0=n_prompt, seq=dec_seq, chunk=dec_seq, group=1, name="gla_sample")

        brow = jnp.pad(attn_bias[l][:, ::-1],
                       ((0, 0), (B_LEFT - CHUNK, BIAS_ROW - (B_LEFT - CHUNK) - (2 * REL_CLIP + 1))), mode="edge")
        brow = (brow - brow[:, :1]) * LOG2E
        gain = jnp.tile(row(attn_norm[l]), (1, 4))
        a_p = _attn_prompt_call(bq, bk, bv, brow, gain, n_streams=batch, seq=seq)
        a_s = _attn_sample_call(bq, bk, bv, ck_t, cv_t, brow, gain,
                                layer=l, row0=n_prompt, tq=dec_seq)

        x = _ffn_call(x, row(norm_ffn2[l]), *ffn2, layer=l, mix=((g_p, g_s), (a_p, a_s), wo),
                      final_norm=row(final_norm) if l == DEPTH - 1 else None, **sizes)
        gla_p.append(sp.reshape(batch, A_HEADS, A_DK, A_DV))
        gla_s.append(ss.reshape(dec_batch, A_HEADS, A_DK, A_DV))

    norms = norm_mix.reshape(DEPTH, 1, D_MODEL).astype(F32)
    w_kv = w_in[:, :, o_b + B_W:]
    tiles_per_stream = seq // TM
    k_p, v_p = _kv_tails_call(proj_in, norms, jnp.swapaxes(w_kv, 1, 2).astype(BF), first_tile=tiles_per_stream - 1,
                              tile_step=tiles_per_stream, n_tiles=batch, transposed=True)
    k_s, v_s = _kv_tails_call(proj_in, norms, w_kv.astype(BF), first_tile=n_prompt // TM, tile_step=1,
                              n_tiles=n_sample // TM, transposed=False)
    tail_rows = lambda t: jnp.transpose(t, (0, 1, 4, 2, 3))
    y_prompt, y_sample = x
    return (y_prompt.reshape(batch, seq, D_MODEL), y_sample.reshape(dec_batch, dec_seq, D_MODEL),
            jnp.stack(gla_p), tail_rows(k_p), tail_rows(v_p), jnp.stack(gla_s),
            k_s.reshape(DEPTH, dec_batch, dec_seq, B_HEADS, B_HD),
            v_s.reshape(DEPTH, dec_batch, dec_seq, B_HEADS, B_HD))
```
